```python
import math
import jax, jax.numpy as jnp
from jax import lax
import numpy as np

D_MODEL = 2048
BATCH = 4
SEQ = 2048
DEPTH = 2

N_MIXERS = 2
N_SSD_LAYERS = (DEPTH + 1) // 2
N_FOX_LAYERS = DEPTH // 2
EPS = 1e-6

SSD_EXPAND = 2
SSD_D_INNER = SSD_EXPAND * D_MODEL
SSD_HEAD_DIM = 64
SSD_N_HEADS = SSD_D_INNER // SSD_HEAD_DIM
SSD_N_GROUPS = 8
SSD_HEADS_PER_GROUP = SSD_N_HEADS // SSD_N_GROUPS
SSD_D_STATE = 128
SSD_CONV_K = 4
SSD_CHUNK = 128
SSD_CONV_DIM = SSD_D_INNER + 2 * SSD_N_GROUPS * SSD_D_STATE
SSD_IN_DIM = SSD_D_INNER + SSD_CONV_DIM + SSD_N_HEADS

FOX_N_HEADS = 16
FOX_HEAD_DIM = 128
FOX_WIDTH = FOX_N_HEADS * FOX_HEAD_DIM
FOX_IN_DIM = 3 * FOX_WIDTH + FOX_N_HEADS
FOX_Q_BLOCK = 128

FFN_HIDDEN = int(math.ceil(8 * D_MODEL / 3 / 256) * 256)

PLE_DIM = 256

kernel_name = "hybrid_ssd_fox_sandwich_ple"


def rmsnorm(x, w):
    xf = x.astype(jnp.float32)
    y = xf * lax.rsqrt(jnp.mean(xf * xf, axis=-1, keepdims=True) + EPS)
    return (y * w.astype(jnp.float32)).astype(x.dtype)


def causal_depthwise_conv(x, w, b):
    k = w.shape[0]
    out = lax.conv_general_dilated(
        x, w[:, None, :].astype(x.dtype), window_strides=(1,), padding=[(k - 1, 0)],
        dimension_numbers=("NWC", "WIO", "NWC"), feature_group_count=x.shape[-1])
    return out + b.astype(x.dtype)


def ssd_chunked_scan(x, dt, a, bm, cm):
    b, l, g, r, p = x.shape
    n = bm.shape[-1]
    nc = l // SSD_CHUNK
    q = SSD_CHUNK
    x = x.reshape(b, nc, q, g, r, p)
    dt = dt.reshape(b, nc, q, g, r)
    bm = bm.reshape(b, nc, q, g, n)
    cm = cm.reshape(b, nc, q, g, n)
    xdt = x * dt[..., None]
    a_dt = (dt * a).transpose(0, 3, 4, 1, 2)
    a_cum = jnp.cumsum(a_dt, axis=-1)
    seg = a_cum[..., :, None] - a_cum[..., None, :]
    tril = jnp.tril(jnp.ones((q, q), dtype=bool))
    decay_l = jnp.exp(jnp.where(tril, seg, -jnp.inf))
    cb = jnp.einsum("bclgn,bcsgn->bcgls", cm, bm)
    y_diag = jnp.einsum("bcgls,bgrcls,bcsgrp->bclgrp", cb, decay_l, xdt)
    decay_states = jnp.exp(a_cum[..., -1:] - a_cum)
    states = jnp.einsum("bcsgn,bgrcs,bcsgrp->cbgrpn", bm, decay_states, xdt)
    chunk_decay = jnp.moveaxis(jnp.exp(a_cum[..., -1]), -1, 0)

    def step(h, inp):
        st, dec = inp
        return h * dec[..., None, None] + st, h

    h0 = jnp.zeros((b, g, r, p, n), dtype=jnp.float32)
    _, prev = lax.scan(step, h0, (states, chunk_decay))
    y_off = jnp.einsum("bclgn,cbgrpn,bgrcl->bclgrp", cm, prev, jnp.exp(a_cum))
    return (y_diag + y_off).reshape(b, l, g, r, p)


def ssd_mixer(u, w_in, conv_w, conv_b, dt_bias, a_log, d_skip, norm_w, w_out):
    b, l, _ = u.shape
    g, r, pd, n = SSD_N_GROUPS, SSD_HEADS_PER_GROUP, SSD_HEAD_DIM, SSD_D_STATE
    proj = u @ w_in
    z = proj[..., :SSD_D_INNER]
    xbc = proj[..., SSD_D_INNER:SSD_D_INNER + SSD_CONV_DIM]
    dt = proj[..., SSD_D_INNER + SSD_CONV_DIM:]
    xbc = jax.nn.silu(causal_depthwise_conv(xbc, conv_w, conv_b))
    xs = xbc[..., :SSD_D_INNER].astype(jnp.float32).reshape(b, l, g, r, pd)
    bm = xbc[..., SSD_D_INNER:SSD_D_INNER + g * n].astype(jnp.float32).reshape(b, l, g, n)
    cm = xbc[..., SSD_D_INNER + g * n:].astype(jnp.float32).reshape(b, l, g, n)
    dt = jax.nn.softplus(dt.astype(jnp.float32) + dt_bias.astype(jnp.float32)).reshape(b, l, g, r)
    a = -jnp.exp(a_log.astype(jnp.float32)).reshape(g, r)
    y = ssd_chunked_scan(xs, dt, a, bm, cm)
    y = y + xs * d_skip.astype(jnp.float32).reshape(g, r, 1)
    y = y.reshape(b, l, SSD_D_INNER) * jax.nn.silu(z.astype(jnp.float32))
    yg = y.reshape(b, l, g, SSD_D_INNER // g)
    yg = yg * lax.rsqrt(jnp.mean(yg * yg, axis=-1, keepdims=True) + EPS)
    y = yg.reshape(b, l, SSD_D_INNER) * norm_w.astype(jnp.float32)
    return y.astype(u.dtype) @ w_out


def fox_mixer(u, w_in, b_f, w_out):
    b, l, _ = u.shape
    hn, hd = FOX_N_HEADS, FOX_HEAD_DIM
    proj = u @ w_in
    q = proj[..., :FOX_WIDTH].reshape(b, l, hn, hd) * (hd ** -0.5)
    k = proj[..., FOX_WIDTH:2 * FOX_WIDTH].reshape(b, l, hn, hd)
    v = proj[..., 2 * FOX_WIDTH:3 * FOX_WIDTH].reshape(b, l, hn, hd)
    log_f = jax.nn.log_sigmoid(proj[..., 3 * FOX_WIDTH:].astype(jnp.float32) + b_f.astype(jnp.float32))
    csum = jnp.cumsum(log_f, axis=1).transpose(0, 2, 1)
    outs = []
    for blk in range(l // FOX_Q_BLOCK):
        lo = blk * FOX_Q_BLOCK
        hi = lo + FOX_Q_BLOCK
        s = jnp.einsum("bqhd,bkhd->bhqk", q[:, lo:hi], k[:, :hi],
                       preferred_element_type=jnp.float32)
        s = s + csum[:, :, lo:hi, None] - csum[:, :, None, :hi]
        mask = jnp.arange(lo, hi)[:, None] >= jnp.arange(hi)[None, :]
        s = jnp.where(mask, s, -jnp.inf)
        pr = jax.nn.softmax(s, axis=-1).astype(v.dtype)
        outs.append(jnp.einsum("bhqk,bkhd->bqhd", pr, v[:, :hi]))
    o = jnp.concatenate(outs, axis=1).reshape(b, l, FOX_WIDTH)
    return o @ w_out


def swiglu(h, w_gate, w_up, w_down):
    return (jax.nn.silu(h @ w_gate) * (h @ w_up)) @ w_down


def setup_inputs(seed: int = 0) -> dict:
    key = jax.random.key(seed)
    ks = iter(jax.random.split(key, 32))
    f32 = jnp.float32

    def normal(shape, fan_in):
        return jax.random.normal(next(ks), shape, f32) * (fan_in ** -0.5)

    def gain(shape):
        return 1.0 + 0.02 * jax.random.normal(next(ks), shape, f32)

    x = jax.random.normal(next(ks), (BATCH, SEQ, D_MODEL), f32)
    p = jax.random.normal(next(ks), (DEPTH, BATCH, SEQ, PLE_DIM), f32)

    norm_mix_pre = gain((DEPTH, D_MODEL))
    norm_mix_post = gain((DEPTH, D_MODEL))
    norm_ffn_pre = gain((DEPTH, D_MODEL))
    norm_ffn_post = gain((DEPTH, D_MODEL))

    ssd_w_in = normal((N_SSD_LAYERS, D_MODEL, SSD_IN_DIM), D_MODEL)
    ssd_conv_w = normal((N_SSD_LAYERS, SSD_CONV_K, SSD_CONV_DIM), SSD_CONV_K)
    ssd_conv_b = 0.02 * jax.random.normal(next(ks), (N_SSD_LAYERS, SSD_CONV_DIM), f32)
    u = jax.random.uniform(next(ks), (N_SSD_LAYERS, SSD_N_HEADS), f32)
    dt0 = jnp.exp(u * (math.log(0.1) - math.log(0.001)) + math.log(0.001))
    ssd_dt_bias = dt0 + jnp.log(-jnp.expm1(-dt0))
    ssd_a_log = jnp.log(jax.random.uniform(next(ks), (N_SSD_LAYERS, SSD_N_HEADS), f32, 1.0, 16.0))
    ssd_d = gain((N_SSD_LAYERS, SSD_N_HEADS))
    ssd_norm_w = gain((N_SSD_LAYERS, SSD_D_INNER))
    ssd_w_out = normal((N_SSD_LAYERS, SSD_D_INNER, D_MODEL), SSD_D_INNER)

    fox_w_in = normal((N_FOX_LAYERS, D_MODEL, FOX_IN_DIM), D_MODEL)
    fox_b_f = jax.random.uniform(next(ks), (N_FOX_LAYERS, FOX_N_HEADS), f32, 1.0, 4.0)
    fox_w_out = normal((N_FOX_LAYERS, FOX_WIDTH, D_MODEL), FOX_WIDTH)

    ffn_w_gate = normal((DEPTH, D_MODEL, FFN_HIDDEN), D_MODEL)
    ffn_w_up = normal((DEPTH, D_MODEL, FFN_HIDDEN), D_MODEL)
    ffn_w_down = normal((DEPTH, FFN_HIDDEN, D_MODEL), FFN_HIDDEN)

    ple_w_proj = normal((DEPTH, PLE_DIM, D_MODEL), PLE_DIM)
    ple_norm = gain((DEPTH, D_MODEL))
    ple_w_gate = normal((DEPTH, D_MODEL, D_MODEL), D_MODEL)

    return {
        "x": x, "p": p,
        "norm_mix_pre": norm_mix_pre, "norm_mix_post": norm_mix_post,
        "norm_ffn_pre": norm_ffn_pre, "norm_ffn_post": norm_ffn_post,
        "ssd_w_in": ssd_w_in, "ssd_conv_w": ssd_conv_w, "ssd_conv_b": ssd_conv_b,
        "ssd_dt_bias": ssd_dt_bias, "ssd_a_log": ssd_a_log, "ssd_d": ssd_d,
        "ssd_norm_w": ssd_norm_w, "ssd_w_out": ssd_w_out,
        "fox_w_in": fox_w_in, "fox_b_f": fox_b_f, "fox_w_out": fox_w_out,
        "ffn_w_gate": ffn_w_gate, "ffn_w_up": ffn_w_up, "ffn_w_down": ffn_w_down,
        "ple_w_proj": ple_w_proj, "ple_norm": ple_norm, "ple_w_gate": ple_w_gate,
    }


def reference(x, p, norm_mix_pre, norm_mix_post, norm_ffn_pre, norm_ffn_post,
              ssd_w_in, ssd_conv_w, ssd_conv_b, ssd_dt_bias, ssd_a_log, ssd_d,
              ssd_norm_w, ssd_w_out, fox_w_in, fox_b_f, fox_w_out,
              ffn_w_gate, ffn_w_up, ffn_w_down, ple_w_proj, ple_norm, ple_w_gate):
    h = x
    for i in range(DEPTH):
        j = i // N_MIXERS
        hn = rmsnorm(h, norm_mix_pre[i])
        if i % N_MIXERS == 0:
            mix = ssd_mixer(hn, ssd_w_in[j], ssd_conv_w[j], ssd_conv_b[j], ssd_dt_bias[j],
                            ssd_a_log[j], ssd_d[j], ssd_norm_w[j], ssd_w_out[j])
        else:
            mix = fox_mixer(hn, fox_w_in[j], fox_b_f[j], fox_w_out[j])
        h = h + rmsnorm(mix, norm_mix_post[i])
        ff = swiglu(rmsnorm(h, norm_ffn_pre[i]), ffn_w_gate[i], ffn_w_up[i], ffn_w_down[i])
        h = h + rmsnorm(ff, norm_ffn_post[i])
        pe = rmsnorm(p[i] @ ple_w_proj[i], ple_norm[i])
        h = h + jax.nn.sigmoid(h @ ple_w_gate[i]) * pe
    return h
```

```python
import functools
import math

import jax
import jax.numpy as jnp
from jax import lax
from jax.experimental import pallas as pl
from jax.experimental.pallas import tpu as pltpu

f32 = jnp.float32
bf16 = jnp.bfloat16

D_MODEL = 2048
EPS = 1e-6

SSD_D_INNER = 4096
SSD_HEAD_DIM = 64
SSD_N_HEADS = 64
SSD_N_GROUPS = 8
SSD_HEADS_PER_GROUP = 8
SSD_D_STATE = 128
SSD_CONV_K = 4
SSD_CHUNK = 128
SSD_GROUP_WIDTH = SSD_HEADS_PER_GROUP * SSD_HEAD_DIM
SSD_BC_WIDTH = SSD_N_GROUPS * SSD_D_STATE
SSD_CONV_DIM = SSD_D_INNER + 2 * SSD_BC_WIDTH
SSD_ZX_DIM = SSD_D_INNER + SSD_CONV_DIM

FOX_N_HEADS = 16
FOX_HEAD_DIM = 128
FOX_WIDTH = 2048

FFN_HIDDEN = 5632
PLE_DIM = 256

LANES = 128
VMEM_LIMIT_BYTES = 48 * 1024 * 1024


def _params(*semantics):
    return pltpu.CompilerParams(dimension_semantics=semantics, vmem_limit_bytes=VMEM_LIMIT_BYTES)


def _dot(a, b):
    return jnp.dot(a, b, preferred_element_type=f32)


def _dot_nt(a, b):
    return lax.dot_general(a, b, (((1,), (1,)), ((), ())), preferred_element_type=f32)


def _dot_tn(a, b):
    return lax.dot_general(a, b, (((0,), (0,)), ((), ())), preferred_element_type=f32)


def _rms(x, gain):
    return x * lax.rsqrt(jnp.mean(x * x, axis=-1, keepdims=True) + EPS) * gain


def _sigmoid(x):
    return 1.0 / (1.0 + jnp.exp(-x))


def _softplus(x):
    return jnp.maximum(x, 0.0) + jnp.log1p(jnp.exp(-jnp.abs(x)))


def _split3(x):
    hi = x.astype(bf16)
    r1 = x - hi.astype(f32)
    mid = r1.astype(bf16)
    lo = (r1 - mid.astype(f32)).astype(bf16)
    return hi, mid, lo


def _tri(n, lower):
    r = lax.broadcasted_iota(jnp.int32, (n, n), 0)
    c = lax.broadcasted_iota(jnp.int32, (n, n), 1)
    return jnp.where((r >= c) if lower else (r <= c), 1.0, 0.0).astype(bf16)


def _rmsnorm_kernel(x_ref, g_ref, o_ref):
    o_ref[...] = _rms(x_ref[...], g_ref[...]).astype(o_ref.dtype)


def _rmsnorm_bf16(x, gain, tm=512):
    m, d = x.shape
    return pl.pallas_call(
        _rmsnorm_kernel,
        grid=(m // tm,),
        in_specs=[pl.BlockSpec((tm, d), lambda i: (i, 0)), pl.BlockSpec((1, d), lambda i: (0, 0))],
        out_specs=pl.BlockSpec((tm, d), lambda i: (i, 0)),
        out_shape=jax.ShapeDtypeStruct((m, d), bf16),
        compiler_params=_params("parallel"),
        name="rmsnorm",
    )(x, gain.reshape(1, d))


def _mm_kernel(x_ref, w_ref, o_ref):
    o_ref[...] = _dot(x_ref[...], w_ref[...]).astype(o_ref.dtype)


def _matmul(x, w, tm, tn, name):
    m, k = x.shape
    n = w.shape[1]
    return pl.pallas_call(
        _mm_kernel,
        grid=(m // tm, n // tn),
        in_specs=[pl.BlockSpec((tm, k), lambda i, j: (i, 0)), pl.BlockSpec((k, tn), lambda i, j: (0, j))],
        out_specs=pl.BlockSpec((tm, tn), lambda i, j: (i, j)),
        out_shape=jax.ShapeDtypeStruct((m, n), bf16),
        compiler_params=_params("parallel", "arbitrary"),
        name=name,
    )(x, w)


def _next_activation(hnew, gnext_ref, next_mode):
    if next_mode == "norm":
        return _rms(hnew, gnext_ref[...]).astype(bf16)
    return hnew.astype(bf16)


def _mm_res_kernel(x_ref, w_ref, h_ref, gpost_ref, gnext_ref, hout_ref, hn_ref, *, next_mode):
    mix = _dot(x_ref[...], w_ref[...])
    hnew = h_ref[...] + _rms(mix, gpost_ref[...])
    hout_ref[...] = hnew
    hn_ref[...] = _next_activation(hnew, gnext_ref, next_mode)


def _matmul_residual(x, w, h, gpost, gnext, next_mode, name, tm=256):
    m, k = x.shape
    d = w.shape[1]
    row = lambda i: (i, 0)
    fixed = lambda i: (0, 0)
    return pl.pallas_call(
        functools.partial(_mm_res_kernel, next_mode=next_mode),
        grid=(m // tm,),
        in_specs=[
            pl.BlockSpec((tm, k), row),
            pl.BlockSpec((k, d), fixed, pipeline_mode=pl.Buffered(1)),
            pl.BlockSpec((tm, d), row),
            pl.BlockSpec((1, d), fixed),
            pl.BlockSpec((1, d), fixed),
        ],
        out_specs=[pl.BlockSpec((tm, d), row), pl.BlockSpec((tm, d), row)],
        out_shape=[jax.ShapeDtypeStruct((m, d), f32), jax.ShapeDtypeStruct((m, d), bf16)],
        compiler_params=_params("parallel"),
        name=name,
    )(x, w, h, gpost.reshape(1, d), gnext.reshape(1, d))


def _ffn_up_kernel(x_ref, wg_ref, wu_ref, o_ref):
    x = x_ref[...]
    g = _dot(x, wg_ref[...])
    u = _dot(x, wu_ref[...])
    o_ref[...] = (g * _sigmoid(g) * u).astype(o_ref.dtype)


def _ffn_up(x, wg, wu, tm=1024, tn=512):
    m, k = x.shape
    n = wg.shape[1]
    return pl.pallas_call(
        _ffn_up_kernel,
        grid=(m // tm, n // tn),
        in_specs=[
            pl.BlockSpec((tm, k), lambda i, j: (i, 0)),
            pl.BlockSpec((k, tn), lambda i, j: (0, j)),
            pl.BlockSpec((k, tn), lambda i, j: (0, j)),
        ],
        out_specs=pl.BlockSpec((tm, tn), lambda i, j: (i, j)),
        out_shape=jax.ShapeDtypeStruct((m, n), bf16),
        compiler_params=_params("parallel", "arbitrary"),
        name="ffn_up",
    )(x, wg, wu)


def _ple_kernel(hb_ref, h_ref, p_ref, wgate_ref, wproj_ref, gple_ref, gnext_ref, *out_refs, next_mode):
    gate = _dot(hb_ref[...], wgate_ref[...])
    pe = _rms(_dot(p_ref[...].astype(bf16), wproj_ref[...]), gple_ref[...])
    hnew = h_ref[...] + _sigmoid(gate) * pe
    out_refs[0][...] = hnew
    if next_mode != "none":
        out_refs[1][...] = _next_activation(hnew, gnext_ref, next_mode)


def _ple(hb, h, p, wgate, wproj, gple, gnext, next_mode, tm=256):
    m, d = h.shape
    pd = p.shape[1]
    row = lambda i: (i, 0)
    fixed = lambda i: (0, 0)
    out_specs = [pl.BlockSpec((tm, d), row)]
    out_shape = [jax.ShapeDtypeStruct((m, d), f32)]
    if next_mode != "none":
        out_specs.append(pl.BlockSpec((tm, d), row))
        out_shape.append(jax.ShapeDtypeStruct((m, d), bf16))
    return pl.pallas_call(
        functools.partial(_ple_kernel, next_mode=next_mode),
        grid=(m // tm,),
        in_specs=[
            pl.BlockSpec((tm, d), row),
            pl.BlockSpec((tm, d), row),
            pl.BlockSpec((tm, pd), row),
            pl.BlockSpec((d, d), fixed, pipeline_mode=pl.Buffered(1)),
            pl.BlockSpec((pd, d), fixed, pipeline_mode=pl.Buffered(1)),
            pl.BlockSpec((1, d), fixed),
            pl.BlockSpec((1, d), fixed),
        ],
        out_specs=out_specs,
        out_shape=out_shape,
        compiler_params=_params("parallel"),
        name="ple",
    )(hb, h, p, wgate, wproj, gple.reshape(1, d), gnext.reshape(1, d))


def _ssd_dt_kernel(hn_ref, w_ref, wt_ref, bias_row_ref, alog_row_ref, bias_col_ref, alog_col_ref,
                   dt_ref, acum_ref, acum_t_ref):
    hn = hn_ref[...]
    tm = hn.shape[0]
    q = SSD_CHUNK
    dt = _softplus(_dot(hn, w_ref[...]) + bias_row_ref[...])
    dt_ref[...] = dt
    adt = dt * (-jnp.exp(alog_row_ref[...]))
    tril = _tri(q, lower=True)
    for c in range(tm // q):
        hi, mid, lo = _split3(adt[c * q:(c + 1) * q, :])
        acum_ref[c * q:(c + 1) * q, :] = _dot(tril, hi) + _dot(tril, mid) + _dot(tril, lo)
    dt_t = _softplus(_dot_nt(wt_ref[...], hn) + bias_col_ref[...])
    adt_t = dt_t * (-jnp.exp(alog_col_ref[...]))
    triu = _tri(q, lower=False)
    for c in range(tm // q):
        hi, mid, lo = _split3(adt_t[:, c * q:(c + 1) * q])
        acum_t_ref[:, c * q:(c + 1) * q] = _dot(hi, triu) + _dot(mid, triu) + _dot(lo, triu)


def _ssd_dt(hn, w_dt, dt_bias, a_log, tm=1024):
    m, d = hn.shape
    nh = SSD_N_HEADS
    w_pad = jnp.pad(w_dt, ((0, 0), (0, LANES - nh))).astype(bf16)
    w_t = w_dt.T.astype(bf16)
    pad_row = lambda v: jnp.pad(v, (0, LANES - nh)).reshape(1, LANES)
    fixed = lambda i: (0, 0)
    return pl.pallas_call(
        _ssd_dt_kernel,
        grid=(m // tm,),
        in_specs=[
            pl.BlockSpec((tm, d), lambda i: (i, 0)),
            pl.BlockSpec((d, LANES), fixed),
            pl.BlockSpec((nh, d), fixed),
            pl.BlockSpec((1, LANES), fixed),
            pl.BlockSpec((1, LANES), fixed),
            pl.BlockSpec((nh, 1), fixed),
            pl.BlockSpec((nh, 1), fixed),
        ],
        out_specs=[
            pl.BlockSpec((tm, LANES), lambda i: (i, 0)),
            pl.BlockSpec((tm, LANES), lambda i: (i, 0)),
            pl.BlockSpec((nh, tm), lambda i: (0, i)),
        ],
        out_shape=[
            jax.ShapeDtypeStruct((m, LANES), f32),
            jax.ShapeDtypeStruct((m, LANES), f32),
            jax.ShapeDtypeStruct((nh, m), f32),
        ],
        compiler_params=_params("parallel"),
        name="ssd_dt",
    )(hn, w_pad, w_t, pad_row(dt_bias), pad_row(a_log), dt_bias.reshape(nh, 1), a_log.reshape(nh, 1))


CONV_HALO = 8
CONV_ROWS = 256


def _conv_kernel(x_ref, w_ref, b_ref, o_ref, pad_ref):
    seq = x_ref.shape[0]
    pad_ref[0:CONV_HALO, :] = jnp.zeros((CONV_HALO, pad_ref.shape[1]), f32)
    pad_ref[CONV_HALO:CONV_HALO + seq, :] = x_ref[...].astype(f32)
    for r in range(seq // CONV_ROWS):
        base = CONV_HALO + r * CONV_ROWS
        acc = b_ref[...] + w_ref[SSD_CONV_K - 1:SSD_CONV_K, :] * pad_ref[base:base + CONV_ROWS, :]
        for j in range(SSD_CONV_K - 1):
            shift = SSD_CONV_K - 1 - j
            acc = acc + w_ref[j:j + 1, :] * pad_ref[base - shift:base - shift + CONV_ROWS, :]
        o_ref[r * CONV_ROWS:(r + 1) * CONV_ROWS, :] = (acc * _sigmoid(acc)).astype(o_ref.dtype)


def _ssd_conv(zx, conv_w, conv_b, batch, seq, tc=512):
    m = zx.shape[0]
    col0 = SSD_D_INNER // tc
    return pl.pallas_call(
        _conv_kernel,
        grid=(batch, SSD_CONV_DIM // tc),
        in_specs=[
            pl.BlockSpec((seq, tc), lambda b, j: (b, col0 + j)),
            pl.BlockSpec((SSD_CONV_K, tc), lambda b, j: (0, j)),
            pl.BlockSpec((1, tc), lambda b, j: (0, j)),
        ],
        out_specs=pl.BlockSpec((seq, tc), lambda b, j: (b, j)),
        out_shape=jax.ShapeDtypeStruct((m, SSD_CONV_DIM), bf16),
        scratch_shapes=[pltpu.VMEM((CONV_HALO + seq, tc), f32)],
        compiler_params=_params("parallel", "parallel"),
        name="ssd_conv",
    )(zx, conv_w, conv_b.reshape(1, SSD_CONV_DIM))


def _ssd_scan_kernel(xs_ref, b_ref, c_ref, z_ref, dt_ref, acum_ref, acum_t_ref, dskip_ref, normw_ref,
                     o_ref, state_ref):
    q = SSD_CHUNK
    gw = SSD_GROUP_WIDTH
    hd = SSD_HEAD_DIM

    @pl.when(pl.program_id(1) == 0)
    def _():
        state_ref[...] = jnp.zeros_like(state_ref)

    rows = lax.broadcasted_iota(jnp.int32, (q, q), 0)
    cols = lax.broadcasted_iota(jnp.int32, (q, q), 1)
    tril = rows >= cols
    first_head = cols < hd

    for g in range(SSD_N_GROUPS):
        bg = b_ref[:, g * SSD_D_STATE:(g + 1) * SSD_D_STATE]
        cg = c_ref[:, g * SSD_D_STATE:(g + 1) * SSD_D_STATE]
        cb = _dot_nt(cg, bg)
        xg = xs_ref[:, g * gw:(g + 1) * gw].astype(f32)

        acol = []
        dt_pairs = []
        ac_pairs = []
        for i in range(SSD_HEADS_PER_GROUP):
            h = g * SSD_HEADS_PER_GROUP + i
            acol.append(jnp.broadcast_to(acum_ref[:, h:h + 1], (q, q)))
        for pr in range(SSD_HEADS_PER_GROUP // 2):
            h0 = g * SSD_HEADS_PER_GROUP + 2 * pr
            d0 = jnp.broadcast_to(dt_ref[:, h0:h0 + 1], (q, q))
            d1 = jnp.broadcast_to(dt_ref[:, h0 + 1:h0 + 2], (q, q))
            dt_pairs.append(jnp.where(first_head, d0, d1))
            ac_pairs.append(jnp.where(first_head, acol[2 * pr], acol[2 * pr + 1]))
        dt_g = jnp.concatenate(dt_pairs, axis=1)
        ac_g = jnp.concatenate(ac_pairs, axis=1)
        a_last = ac_g[q - 1:q, :]

        xdt = xg * dt_g
        st_new = _dot_tn(bg, (xdt * jnp.exp(a_last - ac_g)).astype(bf16))
        prev = state_ref[g]
        y = _dot(cg, prev.astype(bf16)) * jnp.exp(ac_g)
        state_ref[g] = prev * jnp.exp(a_last) + st_new

        y_pairs = []
        for pr in range(SSD_HEADS_PER_GROUP // 2):
            xp = xdt[:, pr * q:(pr + 1) * q]
            rhs = jnp.concatenate(
                [jnp.where(first_head, xp, 0.0).astype(bf16), jnp.where(first_head, 0.0, xp).astype(bf16)], axis=0)
            lhs = []
            for k in range(2):
                i = 2 * pr + k
                h = g * SSD_HEADS_PER_GROUP + i
                seg = acol[i] - acum_t_ref[h:h + 1, :]
                decay = jnp.exp(jnp.where(tril, seg, -jnp.inf))
                lhs.append((cb * decay).astype(bf16))
            y_pairs.append(_dot(jnp.concatenate(lhs, axis=1), rhs))
        y = y + jnp.concatenate(y_pairs, axis=1) + xg * dskip_ref[:, g * gw:(g + 1) * gw]

        zg = z_ref[:, g * gw:(g + 1) * gw].astype(f32)
        y = y * (zg * _sigmoid(zg))
        o_ref[:, g * gw:(g + 1) * gw] = _rms(y, normw_ref[:, g * gw:(g + 1) * gw]).astype(o_ref.dtype)


def _ssd_scan(zx, xbc, dt, acum, acum_t, d_skip, norm_w, batch, seq):
    m = zx.shape[0]
    q = SSD_CHUNK
    nc = seq // q
    di = SSD_D_INNER
    bw = SSD_BC_WIDTH
    chunk = lambda b, c: (b * nc + c, 0)
    fixed = lambda b, c: (0, 0)
    d_cols = jnp.repeat(d_skip, SSD_HEAD_DIM).reshape(1, di)
    return pl.pallas_call(
        _ssd_scan_kernel,
        grid=(batch, nc),
        in_specs=[
            pl.BlockSpec((q, di), chunk),
            pl.BlockSpec((q, bw), lambda b, c: (b * nc + c, di // bw)),
            pl.BlockSpec((q, bw), lambda b, c: (b * nc + c, di // bw + 1)),
            pl.BlockSpec((q, di), chunk),
            pl.BlockSpec((q, LANES), chunk),
            pl.BlockSpec((q, LANES), chunk),
            pl.BlockSpec((SSD_N_HEADS, q), lambda b, c: (0, b * nc + c)),
            pl.BlockSpec((1, di), fixed),
            pl.BlockSpec((1, di), fixed),
        ],
        out_specs=pl.BlockSpec((q, di), chunk),
        out_shape=jax.ShapeDtypeStruct((m, di), bf16),
        scratch_shapes=[pltpu.VMEM((SSD_N_GROUPS, SSD_D_STATE, SSD_GROUP_WIDTH), f32)],
        compiler_params=_params("parallel", "arbitrary"),
        name="ssd_scan",
    )(xbc, xbc, xbc, zx, dt, acum, acum_t, d_cols, norm_w.reshape(1, di))


def _fox_gate_kernel(hn_ref, w_ref, wt_ref, b_row_ref, b_col_ref, csum_ref, csum_t_ref):
    hn = hn_ref[...]
    seq = hn.shape[0]
    q = LANES
    logf = -_softplus(-(_dot(hn, w_ref[...]) + b_row_ref[...]))
    tril = _tri(q, lower=True)
    carry = jnp.zeros((1, LANES), f32)
    for c in range(seq // q):
        hi, mid, lo = _split3(logf[c * q:(c + 1) * q, :])
        cs = _dot(tril, hi) + _dot(tril, mid) + _dot(tril, lo) + carry
        csum_ref[c * q:(c + 1) * q, :] = cs
        carry = cs[q - 1:q, :]
    logf_t = -_softplus(-(_dot_nt(wt_ref[...], hn) + b_col_ref[...]))
    triu = _tri(q, lower=False)
    carry_t = jnp.zeros((FOX_N_HEADS, 1), f32)
    for c in range(seq // q):
        hi, mid, lo = _split3(logf_t[:, c * q:(c + 1) * q])
        cs = _dot(hi, triu) + _dot(mid, triu) + _dot(lo, triu) + carry_t
        csum_t_ref[:, c * q:(c + 1) * q] = cs
        carry_t = cs[:, q - 1:q]


def _fox_gate(hn, w_f, b_f, batch, seq):
    m, d = hn.shape
    nh = FOX_N_HEADS
    w_pad = jnp.pad(w_f, ((0, 0), (0, LANES - nh))).astype(bf16)
    w_t = w_f.T.astype(bf16)
    fixed = lambda b: (0, 0)
    return pl.pallas_call(
        _fox_gate_kernel,
        grid=(batch,),
        in_specs=[
            pl.BlockSpec((seq, d), lambda b: (b, 0)),
            pl.BlockSpec((d, LANES), fixed),
            pl.BlockSpec((nh, d), fixed),
            pl.BlockSpec((1, LANES), fixed),
            pl.BlockSpec((nh, 1), fixed),
        ],
        out_specs=[
            pl.BlockSpec((seq, LANES), lambda b: (b, 0)),
            pl.BlockSpec((None, nh, seq), lambda b: (b, 0, 0)),
        ],
        out_shape=[
            jax.ShapeDtypeStruct((m, LANES), f32),
            jax.ShapeDtypeStruct((batch, nh, seq), f32),
        ],
        compiler_params=_params("parallel"),
        name="fox_gate",
    )(hn, w_pad, w_t, jnp.pad(b_f, (0, LANES - nh)).reshape(1, LANES), b_f.reshape(nh, 1))


ATTN_BLOCK = 256


def _fox_attn_kernel(q_ref, k_ref, v_ref, ccol_ref, crow_ref, o_ref):
    t = ATTN_BLOCK
    head = pl.program_id(1)
    qi = pl.program_id(2)
    q = q_ref[...]
    lane = lax.broadcasted_iota(jnp.int32, (t, LANES), 1)
    ccol = jnp.sum(jnp.where(lane == head, ccol_ref[...], 0.0), axis=1, keepdims=True)
    scale = FOX_HEAD_DIM ** -0.5
    q_pos = qi * t + lax.broadcasted_iota(jnp.int32, (t, t), 0)
    k_off = lax.broadcasted_iota(jnp.int32, (t, t), 1)

    def body(j, carry):
        m_run, l_run, acc = carry
        start = pl.multiple_of(j * t, t)
        k = k_ref[pl.ds(start, t), :]
        v = v_ref[pl.ds(start, t), :]
        s = _dot_nt(q, k) * scale + ccol - crow_ref[j]
        s = jnp.where(q_pos >= k_off + start, s, -jnp.inf)
        m_new = jnp.maximum(m_run, jnp.max(s, axis=1, keepdims=True))
        p = jnp.exp(s - m_new)
        alpha = jnp.exp(m_run - m_new)
        l_new = alpha * l_run + jnp.sum(p, axis=1, keepdims=True)
        acc_new = alpha * acc + _dot(p.astype(bf16), v)
        return m_new, l_new, acc_new

    init = (jnp.full((t, 1), -jnp.inf, f32), jnp.zeros((t, 1), f32), jnp.zeros((t, FOX_HEAD_DIM), f32))
    _, l_fin, acc_fin = lax.fori_loop(0, qi + 1, body, init)
    o_ref[...] = (acc_fin / l_fin).astype(o_ref.dtype)


def _fox_attention(qkv, csum, csum_t, batch, seq):
    m = qkv.shape[0]
    t = ATTN_BLOCK
    nq = seq // t
    nh = FOX_N_HEADS
    hd = FOX_HEAD_DIM
    crow = csum_t.reshape(batch * nh, nq, 1, t)
    return pl.pallas_call(
        _fox_attn_kernel,
        grid=(batch, nh, nq),
        in_specs=[
            pl.BlockSpec((t, hd), lambda b, h, i: (b * nq + i, h)),
            pl.BlockSpec((seq, hd), lambda b, h, i: (b, nh + h)),
            pl.BlockSpec((seq, hd), lambda b, h, i: (b, 2 * nh + h)),
            pl.BlockSpec((t, LANES), lambda b, h, i: (b * nq + i, 0)),
            pl.BlockSpec((None, nq, 1, t), lambda b, h, i: (b * nh + h, 0, 0, 0)),
        ],
        out_specs=pl.BlockSpec((t, hd), lambda b, h, i: (b * nq + i, h)),
        out_shape=jax.ShapeDtypeStruct((m, FOX_WIDTH), bf16),
        compiler_params=_params("parallel", "parallel", "arbitrary"),
        name="fox_attn",
    )(qkv, qkv, qkv, csum, crow)


def kernel(x, p, norm_mix_pre, norm_mix_post, norm_ffn_pre, norm_ffn_post, ssd_w_in, ssd_conv_w, ssd_conv_b, ssd_dt_bias, ssd_a_log, ssd_d, ssd_norm_w, ssd_w_out, fox_w_in, fox_b_f, fox_w_out, ffn_w_gate, ffn_w_up, ffn_w_down, ple_w_proj, ple_norm, ple_w_gate):
    batch, seq, d = x.shape
    depth = p.shape[0]
    m = batch * seq
    h = x.reshape(m, d)
    p2 = p.reshape(depth, m, PLE_DIM)

    hn = _rmsnorm_bf16(h, norm_mix_pre[0])
    for i in range(depth):
        j = i // 2
        if i % 2 == 0:
            w_in = ssd_w_in[j]
            zx = _matmul(hn, w_in[:, :SSD_ZX_DIM].astype(bf16), 1024, 1024, "ssd_in_proj")
            dt, acum, acum_t = _ssd_dt(hn, w_in[:, SSD_ZX_DIM:], ssd_dt_bias[j], ssd_a_log[j])
            xbc = _ssd_conv(zx, ssd_conv_w[j], ssd_conv_b[j], batch, seq)
            mixed = _ssd_scan(zx, xbc, dt, acum, acum_t, ssd_d[j], ssd_norm_w[j], batch, seq)
            w_out = ssd_w_out[j].astype(bf16)
        else:
            w_in = fox_w_in[j]
            qkv = _matmul(hn, w_in[:, :3 * FOX_WIDTH].astype(bf16), 1024, 1024, "fox_in_proj")
            csum, csum_t = _fox_gate(hn, w_in[:, 3 * FOX_WIDTH:], fox_b_f[j], batch, seq)
            mixed = _fox_attention(qkv, csum, csum_t, batch, seq)
            w_out = fox_w_out[j].astype(bf16)
        h, hn = _matmul_residual(mixed, w_out, h, norm_mix_post[i], norm_ffn_pre[i], "norm", "mixer_out_proj")
        act = _ffn_up(hn, ffn_w_gate[i].astype(bf16), ffn_w_up[i].astype(bf16))
        h, hb = _matmul_residual(act, ffn_w_down[i].astype(bf16), h, norm_ffn_post[i], norm_ffn_post[i], "cast",
                                 "ffn_down")
        last = i == depth - 1
        outs = _ple(hb, h, p2[i], ple_w_gate[i].astype(bf16), ple_w_proj[i].astype(bf16), ple_norm[i],
                    norm_mix_pre[0 if last else i + 1], "none" if last else "norm")
        h = outs[0]
        if not last:
            hn = outs[1]
    return h.reshape(batch, seq, d)
```

```python
import functools
import math

import jax
import jax.numpy as jnp
from jax import lax
from jax.experimental import pallas as pl
from jax.experimental.pallas import tpu as pltpu

f32 = jnp.float32
bf16 = jnp.bfloat16

D_MODEL = 2048
EPS = 1e-6

SSD_D_INNER = 4096
SSD_HEAD_DIM = 64
SSD_N_HEADS = 64
SSD_N_GROUPS = 8
SSD_HEADS_PER_GROUP = 8
SSD_D_STATE = 128
SSD_CONV_K = 4
SSD_CHUNK = 128
SSD_GROUP_WIDTH = SSD_HEADS_PER_GROUP * SSD_HEAD_DIM
SSD_BC_WIDTH = SSD_N_GROUPS * SSD_D_STATE
SSD_CONV_DIM = SSD_D_INNER + 2 * SSD_BC_WIDTH
SSD_ZX_DIM = SSD_D_INNER + SSD_CONV_DIM

FOX_N_HEADS = 16
FOX_HEAD_DIM = 128
FOX_WIDTH = 2048

FFN_HIDDEN = 5632
PLE_DIM = 256

LANES = 128
VMEM_LIMIT_BYTES = 48 * 1024 * 1024
VMEM_LIMIT_RESIDENT_BYTES = 58 * 1024 * 1024


def _params(*semantics, vmem=VMEM_LIMIT_BYTES):
    return pltpu.CompilerParams(dimension_semantics=semantics, vmem_limit_bytes=vmem)


def _dot(a, b):
    return jnp.dot(a, b, preferred_element_type=f32)


def _dot_nt(a, b):
    return lax.dot_general(a, b, (((1,), (1,)), ((), ())), preferred_element_type=f32)


def _dot_tn(a, b):
    return lax.dot_general(a, b, (((0,), (0,)), ((), ())), preferred_element_type=f32)


def _rms(x, gain):
    return x * lax.rsqrt(jnp.mean(x * x, axis=-1, keepdims=True) + EPS) * gain


def _sigmoid(x):
    return 1.0 / (1.0 + jnp.exp(-x))


def _softplus(x):
    return jnp.maximum(x, 0.0) + jnp.log1p(jnp.exp(-jnp.abs(x)))


def _split3(x):
    hi = x.astype(bf16)
    r1 = x - hi.astype(f32)
    mid = r1.astype(bf16)
    lo = (r1 - mid.astype(f32)).astype(bf16)
    return hi, mid, lo


def _tri(n, lower):
    r = lax.broadcasted_iota(jnp.int32, (n, n), 0)
    c = lax.broadcasted_iota(jnp.int32, (n, n), 1)
    return jnp.where((r >= c) if lower else (r <= c), 1.0, 0.0).astype(bf16)


def _rmsnorm_kernel(x_ref, g_ref, o_ref):
    o_ref[...] = _rms(x_ref[...], g_ref[...]).astype(o_ref.dtype)


def _rmsnorm_bf16(x, gains, layer, tm=512):
    m, d = x.shape
    return pl.pallas_call(
        _rmsnorm_kernel,
        grid=(m // tm,),
        in_specs=[pl.BlockSpec((tm, d), lambda i: (i, 0)), pl.BlockSpec((None, 1, d), lambda i: (layer, 0, 0))],
        out_specs=pl.BlockSpec((tm, d), lambda i: (i, 0)),
        out_shape=jax.ShapeDtypeStruct((m, d), bf16),
        compiler_params=_params("parallel"),
        name="rmsnorm",
    )(x, gains.reshape(gains.shape[0], 1, d))


def _in_proj_kernel(x_ref, w_ref, scale_ref, o_ref, wb_ref):
    @pl.when(pl.program_id(1) == 0)
    def _():
        wb_ref[...] = w_ref[...].astype(bf16)

    o_ref[...] = (_dot(x_ref[...], wb_ref[...]) * scale_ref[...]).astype(o_ref.dtype)


def _in_proj(x, w_stack, layer, n_cols, col_scale, name, tm=1024, tn=1024):
    m, k = x.shape
    return pl.pallas_call(
        _in_proj_kernel,
        grid=(n_cols // tn, m // tm),
        in_specs=[
            pl.BlockSpec((tm, k), lambda j, i: (i, 0)),
            pl.BlockSpec((None, k, tn), lambda j, i: (layer, 0, j)),
            pl.BlockSpec((1, tn), lambda j, i: (0, j)),
        ],
        out_specs=pl.BlockSpec((tm, tn), lambda j, i: (i, j)),
        out_shape=jax.ShapeDtypeStruct((m, n_cols), bf16),
        scratch_shapes=[pltpu.VMEM((k, tn), bf16)],
        compiler_params=_params("parallel", "arbitrary"),
        name=name,
    )(x, w_stack, col_scale)


def _ffn_up_kernel(x_ref, wg_ref, wu_ref, o_ref, wgb_ref, wub_ref):
    @pl.when(pl.program_id(1) == 0)
    def _():
        wgb_ref[...] = wg_ref[...].astype(bf16)
        wub_ref[...] = wu_ref[...].astype(bf16)

    x = x_ref[...]
    g = _dot(x, wgb_ref[...])
    u = _dot(x, wub_ref[...])
    o_ref[...] = (g * _sigmoid(g) * u).astype(o_ref.dtype)


def _ffn_up(x, wg_stack, wu_stack, layer, tm=1024, tn=512):
    m, k = x.shape
    n = wg_stack.shape[2]
    w_spec = pl.BlockSpec((None, k, tn), lambda j, i: (layer, 0, j))
    return pl.pallas_call(
        _ffn_up_kernel,
        grid=(n // tn, m // tm),
        in_specs=[pl.BlockSpec((tm, k), lambda j, i: (i, 0)), w_spec, w_spec],
        out_specs=pl.BlockSpec((tm, tn), lambda j, i: (i, j)),
        out_shape=jax.ShapeDtypeStruct((m, n), bf16),
        scratch_shapes=[pltpu.VMEM((k, tn), bf16), pltpu.VMEM((k, tn), bf16)],
        compiler_params=_params("parallel", "arbitrary"),
        name="ffn_up",
    )(x, wg_stack, wu_stack)


WEIGHT_CHUNK_ROWS = 256


def _load_weight_bf16(w_hbm, wb_ref, stage_ref, sem_ref):
    rows = WEIGHT_CHUNK_ROWS
    n_chunks = wb_ref.shape[0] // rows

    def chunk_copy(c):
        return pltpu.make_async_copy(w_hbm.at[pl.ds(c * rows, rows)], stage_ref.at[c % 2], sem_ref.at[c % 2])

    chunk_copy(0).start()
    for c in range(n_chunks):
        if c + 1 < n_chunks:
            chunk_copy(c + 1).start()
        chunk_copy(c).wait()
        wb_ref[c * rows:(c + 1) * rows, :] = stage_ref[c % 2].astype(bf16)


def _weight_scratch(k, n):
    return [pltpu.VMEM((k, n), bf16), pltpu.VMEM((2, WEIGHT_CHUNK_ROWS, n), f32), pltpu.SemaphoreType.DMA((2,))]


def _next_activation(hnew, gnext_ref, next_mode):
    if next_mode == "norm":
        return _rms(hnew, gnext_ref[...]).astype(bf16)
    return hnew.astype(bf16)


def _mm_res_kernel(x_ref, w_hbm, h_ref, gpost_ref, gnext_ref, hout_ref, hn_ref, wb_ref, stage_ref, sem_ref, *,
                   layer, next_mode):
    @pl.when(pl.program_id(0) == 0)
    def _():
        _load_weight_bf16(w_hbm.at[layer], wb_ref, stage_ref, sem_ref)

    mix = _dot(x_ref[...], wb_ref[...])
    hnew = h_ref[...] + _rms(mix, gpost_ref[...])
    hout_ref[...] = hnew
    hn_ref[...] = _next_activation(hnew, gnext_ref, next_mode)


def _gain_spec(d, layer):
    return pl.BlockSpec((None, 1, d), lambda i: (layer, 0, 0))


def _gains3(g):
    return g.reshape(g.shape[0], 1, g.shape[1])


def _matmul_residual(x, w_stack, w_layer, h, gpost, gnext, layer, next_mode, name, tm=256):
    m, k = x.shape
    d = h.shape[1]
    row = lambda i: (i, 0)
    return pl.pallas_call(
        functools.partial(_mm_res_kernel, layer=w_layer, next_mode=next_mode),
        grid=(m // tm,),
        in_specs=[
            pl.BlockSpec((tm, k), row),
            pl.BlockSpec(memory_space=pl.ANY),
            pl.BlockSpec((tm, d), row),
            _gain_spec(d, layer),
            _gain_spec(d, layer),
        ],
        out_specs=[pl.BlockSpec((tm, d), row), pl.BlockSpec((tm, d), row)],
        out_shape=[jax.ShapeDtypeStruct((m, d), f32), jax.ShapeDtypeStruct((m, d), bf16)],
        scratch_shapes=_weight_scratch(k, d),
        compiler_params=_params("arbitrary", vmem=VMEM_LIMIT_RESIDENT_BYTES),
        name=name,
    )(x, w_stack, h, _gains3(gpost), _gains3(gnext))


def _ple_kernel(hb_ref, h_ref, p_ref, wgate_hbm, wproj_ref, gple_ref, gnext_ref, *refs, layer, next_mode):
    n_out = 1 if next_mode == "none" else 2
    out_refs = refs[:n_out]
    wb_ref, stage_ref, sem_ref = refs[n_out:]

    @pl.when(pl.program_id(0) == 0)
    def _():
        _load_weight_bf16(wgate_hbm.at[layer], wb_ref, stage_ref, sem_ref)

    gate = _dot(hb_ref[...], wb_ref[...])
    pe = _rms(_dot(p_ref[...].astype(bf16), wproj_ref[...].astype(bf16)), gple_ref[...])
    hnew = h_ref[...] + _sigmoid(gate) * pe
    out_refs[0][...] = hnew
    if next_mode != "none":
        out_refs[1][...] = _next_activation(hnew, gnext_ref, next_mode)


def _ple(hb, h, p_stack, wgate_stack, wproj_stack, gple, gnext, layer, next_layer, next_mode, tm=256):
    m, d = h.shape
    pd = p_stack.shape[2]
    row = lambda i: (i, 0)
    out_specs = [pl.BlockSpec((tm, d), row)]
    out_shape = [jax.ShapeDtypeStruct((m, d), f32)]
    if next_mode != "none":
        out_specs.append(pl.BlockSpec((tm, d), row))
        out_shape.append(jax.ShapeDtypeStruct((m, d), bf16))
    return pl.pallas_call(
        functools.partial(_ple_kernel, layer=layer, next_mode=next_mode),
        grid=(m // tm,),
        in_specs=[
            pl.BlockSpec((tm, d), row),
            pl.BlockSpec((tm, d), row),
            pl.BlockSpec((None, tm, pd), lambda i: (layer, i, 0)),
            pl.BlockSpec(memory_space=pl.ANY),
            pl.BlockSpec((None, pd, d), lambda i: (layer, 0, 0), pipeline_mode=pl.Buffered(1)),
            _gain_spec(d, layer),
            _gain_spec(d, next_layer),
        ],
        out_specs=out_specs,
        out_shape=out_shape,
        scratch_shapes=_weight_scratch(d, d),
        compiler_params=_params("arbitrary", vmem=VMEM_LIMIT_RESIDENT_BYTES),
        name="ple",
    )(hb, h, p_stack, wgate_stack, wproj_stack, _gains3(gple), _gains3(gnext))


def _ssd_dt_kernel(hn_ref, w_ref, wt_ref, bias_row_ref, alog_row_ref, bias_col_ref, alog_col_ref,
                   dt_ref, acum_ref, acum_t_ref):
    hn = hn_ref[...]
    tm = hn.shape[0]
    q = SSD_CHUNK
    dt = _softplus(_dot(hn, w_ref[...]) + bias_row_ref[...])
    dt_ref[...] = dt
    adt = dt * (-jnp.exp(alog_row_ref[...]))
    tril = _tri(q, lower=True)
    for c in range(tm // q):
        hi, mid, lo = _split3(adt[c * q:(c + 1) * q, :])
        acum_ref[c * q:(c + 1) * q, :] = _dot(tril, hi) + _dot(tril, mid) + _dot(tril, lo)
    dt_t = _softplus(_dot_nt(wt_ref[...], hn) + bias_col_ref[...])
    adt_t = dt_t * (-jnp.exp(alog_col_ref[...]))
    triu = _tri(q, lower=False)
    for c in range(tm // q):
        hi, mid, lo = _split3(adt_t[:, c * q:(c + 1) * q])
        acum_t_ref[:, c * q:(c + 1) * q] = _dot(hi, triu) + _dot(mid, triu) + _dot(lo, triu)


def _ssd_dt(hn, w_dt, dt_bias, a_log, tm=1024):
    m, d = hn.shape
    nh = SSD_N_HEADS
    w_pad = jnp.pad(w_dt, ((0, 0), (0, LANES - nh))).astype(bf16)
    w_t = w_dt.T.astype(bf16)
    pad_row = lambda v: jnp.pad(v, (0, LANES - nh)).reshape(1, LANES)
    fixed = lambda i: (0, 0)
    return pl.pallas_call(
        _ssd_dt_kernel,
        grid=(m // tm,),
        in_specs=[
            pl.BlockSpec((tm, d), lambda i: (i, 0)),
            pl.BlockSpec((d, LANES), fixed),
            pl.BlockSpec((nh, d), fixed),
            pl.BlockSpec((1, LANES), fixed),
            pl.BlockSpec((1, LANES), fixed),
            pl.BlockSpec((nh, 1), fixed),
            pl.BlockSpec((nh, 1), fixed),
        ],
        out_specs=[
            pl.BlockSpec((tm, LANES), lambda i: (i, 0)),
            pl.BlockSpec((tm, LANES), lambda i: (i, 0)),
            pl.BlockSpec((nh, tm), lambda i: (0, i)),
        ],
        out_shape=[
            jax.ShapeDtypeStruct((m, LANES), f32),
            jax.ShapeDtypeStruct((m, LANES), f32),
            jax.ShapeDtypeStruct((nh, m), f32),
        ],
        compiler_params=_params("parallel"),
        name="ssd_dt",
    )(hn, w_pad, w_t, pad_row(dt_bias), pad_row(a_log), dt_bias.reshape(nh, 1), a_log.reshape(nh, 1))


CONV_HALO = 8
CONV_ROWS = 256


def _conv_kernel(x_ref, w_ref, b_ref, o_ref, pad_ref):
    seq = x_ref.shape[0]
    pad_ref[0:CONV_HALO, :] = jnp.zeros((CONV_HALO, pad_ref.shape[1]), f32)
    pad_ref[CONV_HALO:CONV_HALO + seq, :] = x_ref[...].astype(f32)
    for r in range(seq // CONV_ROWS):
        base = CONV_HALO + r * CONV_ROWS
        acc = b_ref[...] + w_ref[SSD_CONV_K - 1:SSD_CONV_K, :] * pad_ref[base:base + CONV_ROWS, :]
        for j in range(SSD_CONV_K - 1):
            shift = SSD_CONV_K - 1 - j
            acc = acc + w_ref[j:j + 1, :] * pad_ref[base - shift:base - shift + CONV_ROWS, :]
        o_ref[r * CONV_ROWS:(r + 1) * CONV_ROWS, :] = (acc * _sigmoid(acc)).astype(o_ref.dtype)


def _ssd_conv(zx, conv_w, conv_b, batch, seq, tc=512):
    m = zx.shape[0]
    col0 = SSD_D_INNER // tc
    return pl.pallas_call(
        _conv_kernel,
        grid=(batch, SSD_CONV_DIM // tc),
        in_specs=[
            pl.BlockSpec((seq, tc), lambda b, j: (b, col0 + j)),
            pl.BlockSpec((SSD_CONV_K, tc), lambda b, j: (0, j)),
            pl.BlockSpec((1, tc), lambda b, j: (0, j)),
        ],
        out_specs=pl.BlockSpec((seq, tc), lambda b, j: (b, j)),
        out_shape=jax.ShapeDtypeStruct((m, SSD_CONV_DIM), bf16),
        scratch_shapes=[pltpu.VMEM((CONV_HALO + seq, tc), f32)],
        compiler_params=_params("parallel", "parallel"),
        name="ssd_conv",
    )(zx, conv_w, conv_b.reshape(1, SSD_CONV_DIM))


def _ssd_scan_kernel(xs_ref, b_ref, c_ref, z_ref, dt_ref, acum_ref, acum_t_ref, dskip_ref, normw_ref,
                     o_ref, state_ref):
    q = SSD_CHUNK
    gw = SSD_GROUP_WIDTH
    hd = SSD_HEAD_DIM

    @pl.when(pl.program_id(1) == 0)
    def _():
        state_ref[...] = jnp.zeros_like(state_ref)

    rows = lax.broadcasted_iota(jnp.int32, (q, q), 0)
    cols = lax.broadcasted_iota(jnp.int32, (q, q), 1)
    tril = rows >= cols
    first_head = cols < hd

    for g in range(SSD_N_GROUPS):
        bg = b_ref[:, g * SSD_D_STATE:(g + 1) * SSD_D_STATE]
        cg = c_ref[:, g * SSD_D_STATE:(g + 1) * SSD_D_STATE]
        cb = _dot_nt(cg, bg)
        xg = xs_ref[:, g * gw:(g + 1) * gw].astype(f32)

        acol = []
        dt_pairs = []
        ac_pairs = []
        for i in range(SSD_HEADS_PER_GROUP):
            h = g * SSD_HEADS_PER_GROUP + i
            acol.append(jnp.broadcast_to(acum_ref[:, h:h + 1], (q, q)))
        for pr in range(SSD_HEADS_PER_GROUP // 2):
            h0 = g * SSD_HEADS_PER_GROUP + 2 * pr
            d0 = jnp.broadcast_to(dt_ref[:, h0:h0 + 1], (q, q))
            d1 = jnp.broadcast_to(dt_ref[:, h0 + 1:h0 + 2], (q, q))
            dt_pairs.append(jnp.where(first_head, d0, d1))
            ac_pairs.append(jnp.where(first_head, acol[2 * pr], acol[2 * pr + 1]))
        dt_g = jnp.concatenate(dt_pairs, axis=1)
        ac_g = jnp.concatenate(ac_pairs, axis=1)
        a_last = ac_g[q - 1:q, :]

        xdt = xg * dt_g
        st_new = _dot_tn(bg, (xdt * jnp.exp(a_last - ac_g)).astype(bf16))
        prev = state_ref[g]
        y = _dot(cg, prev.astype(bf16)) * jnp.exp(ac_g)
        state_ref[g] = prev * jnp.exp(a_last) + st_new

        y_pairs = []
        for pr in range(SSD_HEADS_PER_GROUP // 2):
            xp = xdt[:, pr * q:(pr + 1) * q]
            rhs = jnp.concatenate(
                [jnp.where(first_head, xp, 0.0).astype(bf16), jnp.where(first_head, 0.0, xp).astype(bf16)], axis=0)
            lhs = []
            for k in range(2):
                i = 2 * pr + k
                h = g * SSD_HEADS_PER_GROUP + i
                seg = acol[i] - acum_t_ref[h:h + 1, :]
                decay = jnp.exp(jnp.where(tril, seg, -jnp.inf))
                lhs.append((cb * decay).astype(bf16))
            y_pairs.append(_dot(jnp.concatenate(lhs, axis=1), rhs))
        y = y + jnp.concatenate(y_pairs, axis=1) + xg * dskip_ref[:, g * gw:(g + 1) * gw]

        zg = z_ref[:, g * gw:(g + 1) * gw].astype(f32)
        y = y * (zg * _sigmoid(zg))
        o_ref[:, g * gw:(g + 1) * gw] = _rms(y, normw_ref[:, g * gw:(g + 1) * gw]).astype(o_ref.dtype)


def _ssd_scan(zx, xbc, dt, acum, acum_t, d_skip, norm_w, batch, seq):
    m = zx.shape[0]
    q = SSD_CHUNK
    nc = seq // q
    di = SSD_D_INNER
    bw = SSD_BC_WIDTH
    chunk = lambda b, c: (b * nc + c, 0)
    fixed = lambda b, c: (0, 0)
    d_cols = jnp.repeat(d_skip, SSD_HEAD_DIM).reshape(1, di)
    return pl.pallas_call(
        _ssd_scan_kernel,
        grid=(batch, nc),
        in_specs=[
            pl.BlockSpec((q, di), chunk),
            pl.BlockSpec((q, bw), lambda b, c: (b * nc + c, di // bw)),
            pl.BlockSpec((q, bw), lambda b, c: (b * nc + c, di // bw + 1)),
            pl.BlockSpec((q, di), chunk),
            pl.BlockSpec((q, LANES), chunk),
            pl.BlockSpec((q, LANES), chunk),
            pl.BlockSpec((SSD_N_HEADS, q), lambda b, c: (0, b * nc + c)),
            pl.BlockSpec((1, di), fixed),
            pl.BlockSpec((1, di), fixed),
        ],
        out_specs=pl.BlockSpec((q, di), chunk),
        out_shape=jax.ShapeDtypeStruct((m, di), bf16),
        scratch_shapes=[pltpu.VMEM((SSD_N_GROUPS, SSD_D_STATE, SSD_GROUP_WIDTH), f32)],
        compiler_params=_params("parallel", "arbitrary"),
        name="ssd_scan",
    )(xbc, xbc, xbc, zx, dt, acum, acum_t, d_cols, norm_w.reshape(1, di))


LOG2E = math.log2(math.e)


def _fox_gate_kernel(hn_ref, w_ref, b_row_ref, csum_ref):
    hn = hn_ref[...]
    seq = hn.shape[0]
    q = LANES
    logf = -_softplus(-(_dot(hn, w_ref[...]) + b_row_ref[...]))
    tril = _tri(q, lower=True)
    carry = jnp.zeros((1, LANES), f32)
    for c in range(seq // q):
        hi, mid, lo = _split3(logf[c * q:(c + 1) * q, :])
        cs = _dot(tril, hi) + _dot(tril, mid) + _dot(tril, lo) + carry
        csum_ref[c * q:(c + 1) * q, :] = cs * LOG2E
        carry = cs[q - 1:q, :]


def _fox_gate(hn, w_f, b_f, batch, seq):
    m, d = hn.shape
    nh = FOX_N_HEADS
    w_pad = jnp.pad(w_f, ((0, 0), (0, LANES - nh))).astype(bf16)
    fixed = lambda b: (0, 0)
    return pl.pallas_call(
        _fox_gate_kernel,
        grid=(batch,),
        in_specs=[
            pl.BlockSpec((seq, d), lambda b: (b, 0)),
            pl.BlockSpec((d, LANES), fixed),
            pl.BlockSpec((1, LANES), fixed),
        ],
        out_specs=pl.BlockSpec((seq, LANES), lambda b: (b, 0)),
        out_shape=jax.ShapeDtypeStruct((m, LANES), f32),
        compiler_params=_params("parallel"),
        name="fox_gate",
    )(hn, w_pad, jnp.pad(b_f, (0, LANES - nh)).reshape(1, LANES))


ATTN_BLOCK = 512


def _fox_attn_kernel(q_ref, k_ref, v_ref, csum_ref, o_ref, qa_ref, ka_ref):
    seq = q_ref.shape[0]
    t = ATTN_BLOCK
    hd = FOX_HEAD_DIM
    head = pl.program_id(1)
    lane = lax.broadcasted_iota(jnp.int32, (seq, LANES), 1)
    c = jnp.sum(jnp.where(lane == head, csum_ref[...], 0.0), axis=1, keepdims=True)
    hi = c.astype(bf16).astype(f32)
    r1 = c - hi
    mid = r1.astype(bf16).astype(f32)
    lo = r1 - mid
    ext_q = jnp.where(lane == 0, hi, jnp.where(lane == 1, mid, jnp.where(lane == 2, lo,
                                                                         jnp.where(lane < 6, 1.0, 0.0))))
    ext_k = jnp.where(lane < 3, 1.0, jnp.where(lane == 3, -hi, jnp.where(lane == 4, -mid,
                                                                         jnp.where(lane == 5, -lo, 0.0))))
    qa_ref[:, :hd] = q_ref[...]
    qa_ref[:, hd:] = ext_q.astype(bf16)
    ka_ref[:, :hd] = k_ref[...]
    ka_ref[:, hd:] = ext_k.astype(bf16)

    causal = lax.broadcasted_iota(jnp.int32, (t, t), 0) >= lax.broadcasted_iota(jnp.int32, (t, t), 1)
    for qi in range(seq // t):
        q = qa_ref[qi * t:(qi + 1) * t, :]
        for j in range(qi + 1):
            s = _dot_nt(q, ka_ref[j * t:(j + 1) * t, :])
            if j == qi:
                s = jnp.where(causal, s, -jnp.inf)
            v = v_ref[j * t:(j + 1) * t, :]
            m_blk = jnp.max(s, axis=1, keepdims=True)
            if j == 0:
                m_run = m_blk
                p = jnp.exp2(s - m_run)
                l_run = jnp.sum(p, axis=1, keepdims=True)
                acc = _dot(p.astype(bf16), v)
            else:
                m_new = jnp.maximum(m_run, m_blk)
                alpha = jnp.exp2(m_run - m_new)
                p = jnp.exp2(s - m_new)
                l_run = alpha * l_run + jnp.sum(p, axis=1, keepdims=True)
                acc = alpha * acc + _dot(p.astype(bf16), v)
                m_run = m_new
        o_ref[qi * t:(qi + 1) * t, :] = (acc * (1.0 / l_run)).astype(o_ref.dtype)


def _fox_attention(qkv, csum, batch, seq):
    m = qkv.shape[0]
    nh = FOX_N_HEADS
    hd = FOX_HEAD_DIM
    return pl.pallas_call(
        _fox_attn_kernel,
        grid=(batch, nh),
        in_specs=[
            pl.BlockSpec((seq, hd), lambda b, h: (b, h)),
            pl.BlockSpec((seq, hd), lambda b, h: (b, nh + h)),
            pl.BlockSpec((seq, hd), lambda b, h: (b, 2 * nh + h)),
            pl.BlockSpec((seq, LANES), lambda b, h: (b, 0)),
        ],
        out_specs=pl.BlockSpec((seq, hd), lambda b, h: (b, h)),
        out_shape=jax.ShapeDtypeStruct((m, FOX_WIDTH), bf16),
        scratch_shapes=[pltpu.VMEM((seq, 2 * hd), bf16), pltpu.VMEM((seq, 2 * hd), bf16)],
        compiler_params=_params("parallel", "parallel"),
        name="fox_attn",
    )(qkv, qkv, qkv, csum)


def kernel(x, p, norm_mix_pre, norm_mix_post, norm_ffn_pre, norm_ffn_post, ssd_w_in, ssd_conv_w, ssd_conv_b, ssd_dt_bias, ssd_a_log, ssd_d, ssd_norm_w, ssd_w_out, fox_w_in, fox_b_f, fox_w_out, ffn_w_gate, ffn_w_up, ffn_w_down, ple_w_proj, ple_norm, ple_w_gate):
    batch, seq, d = x.shape
    depth = p.shape[0]
    m = batch * seq
    h = x.reshape(m, d)
    p_stack = p.reshape(depth, m, PLE_DIM)
    ssd_scale = jnp.ones((1, SSD_ZX_DIM), f32)
    fox_scale = jnp.concatenate(
        [jnp.full((1, FOX_WIDTH), FOX_HEAD_DIM ** -0.5 * LOG2E, f32), jnp.ones((1, 2 * FOX_WIDTH), f32)], axis=1)

    hn = _rmsnorm_bf16(h, norm_mix_pre, 0)
    for i in range(depth):
        j = i // 2
        if i % 2 == 0:
            zx = _in_proj(hn, ssd_w_in, j, SSD_ZX_DIM, ssd_scale, "ssd_in_proj")
            dt, acum, acum_t = _ssd_dt(hn, ssd_w_in[j][:, SSD_ZX_DIM:], ssd_dt_bias[j], ssd_a_log[j])
            xbc = _ssd_conv(zx, ssd_conv_w[j], ssd_conv_b[j], batch, seq)
            mixed = _ssd_scan(zx, xbc, dt, acum, acum_t, ssd_d[j], ssd_norm_w[j], batch, seq)
            w_out = ssd_w_out
        else:
            qkv = _in_proj(hn, fox_w_in, j, 3 * FOX_WIDTH, fox_scale, "fox_in_proj")
            csum = _fox_gate(hn, fox_w_in[j][:, 3 * FOX_WIDTH:], fox_b_f[j], batch, seq)
            mixed = _fox_attention(qkv, csum, batch, seq)
            w_out = fox_w_out
        h, hn = _matmul_residual(mixed, w_out, j, h, norm_mix_post, norm_ffn_pre, i, "norm", "mixer_out_proj")
        act = _ffn_up(hn, ffn_w_gate, ffn_w_up, i)
        h, hb = _matmul_residual(act, ffn_w_down, i, h, norm_ffn_post, norm_ffn_post, i, "cast", "ffn_down")
        last = i == depth - 1
        outs = _ple(hb, h, p_stack, ple_w_gate, ple_w_proj, ple_norm, norm_mix_pre, i, 0 if last else i + 1,
                    "none" if last else "norm")
        h = outs[0]
        if not last:
            hn = outs[1]
    return h.reshape(batch, seq, d)
```

```python
import functools
import math

import jax
import jax.numpy as jnp
from jax import lax
from jax.experimental import pallas as pl
from jax.experimental.pallas import tpu as pltpu

f32 = jnp.float32
bf16 = jnp.bfloat16

D_MODEL = 2048
EPS = 1e-6

SSD_D_INNER = 4096
SSD_HEAD_DIM = 64
SSD_N_HEADS = 64
SSD_N_GROUPS = 8
SSD_HEADS_PER_GROUP = 8
SSD_D_STATE = 128
SSD_CONV_K = 4
SSD_CHUNK = 128
SSD_GROUP_WIDTH = SSD_HEADS_PER_GROUP * SSD_HEAD_DIM
SSD_BC_WIDTH = SSD_N_GROUPS * SSD_D_STATE
SSD_CONV_DIM = SSD_D_INNER + 2 * SSD_BC_WIDTH
SSD_ZX_DIM = SSD_D_INNER + SSD_CONV_DIM

FOX_N_HEADS = 16
FOX_HEAD_DIM = 128
FOX_WIDTH = 2048

FFN_HIDDEN = 5632
PLE_DIM = 256

LANES = 128
VMEM_LIMIT_BYTES = 48 * 1024 * 1024
VMEM_LIMIT_RESIDENT_BYTES = 58 * 1024 * 1024


def _params(*semantics, vmem=VMEM_LIMIT_BYTES):
    return pltpu.CompilerParams(dimension_semantics=semantics, vmem_limit_bytes=vmem)


def _dot(a, b):
    return jnp.dot(a, b, preferred_element_type=f32)


def _dot_nt(a, b):
    return lax.dot_general(a, b, (((1,), (1,)), ((), ())), preferred_element_type=f32)


def _dot_tn(a, b):
    return lax.dot_general(a, b, (((0,), (0,)), ((), ())), preferred_element_type=f32)


def _rms(x, gain):
    return x * lax.rsqrt(jnp.mean(x * x, axis=-1, keepdims=True) + EPS) * gain


def _sigmoid(x):
    return 1.0 / (1.0 + jnp.exp(-x))


def _softplus(x):
    return jnp.maximum(x, 0.0) + jnp.log1p(jnp.exp(-jnp.abs(x)))


def _split3(x):
    hi = x.astype(bf16)
    r1 = x - hi.astype(f32)
    mid = r1.astype(bf16)
    lo = (r1 - mid.astype(f32)).astype(bf16)
    return hi, mid, lo


def _tri(n, lower):
    r = lax.broadcasted_iota(jnp.int32, (n, n), 0)
    c = lax.broadcasted_iota(jnp.int32, (n, n), 1)
    return jnp.where((r >= c) if lower else (r <= c), 1.0, 0.0).astype(bf16)


def _rmsnorm_kernel(x_ref, g_ref, o_ref):
    o_ref[...] = _rms(x_ref[...], g_ref[...]).astype(o_ref.dtype)


def _rmsnorm_bf16(x, gains, layer, tm=512):
    m, d = x.shape
    return pl.pallas_call(
        _rmsnorm_kernel,
        grid=(m // tm,),
        in_specs=[pl.BlockSpec((tm, d), lambda i: (i, 0)), pl.BlockSpec((None, 1, d), lambda i: (layer, 0, 0))],
        out_specs=pl.BlockSpec((tm, d), lambda i: (i, 0)),
        out_shape=jax.ShapeDtypeStruct((m, d), bf16),
        compiler_params=_params("parallel"),
        name="rmsnorm",
    )(x, gains.reshape(gains.shape[0], 1, d))


def _in_proj_kernel(x_ref, w_ref, scale_ref, o_ref, wb_ref):
    @pl.when(pl.program_id(1) == 0)
    def _():
        wb_ref[...] = w_ref[...].astype(bf16)

    o_ref[...] = (_dot_nt(x_ref[...], wb_ref[...]) * scale_ref[...]).astype(o_ref.dtype)


def _in_proj(x, w_stack, layer, n_cols, col_scale, name, tm=1024, tn=1024):
    m, k = x.shape
    return pl.pallas_call(
        _in_proj_kernel,
        grid=(n_cols // tn, m // tm),
        in_specs=[
            pl.BlockSpec((tm, k), lambda j, i: (i, 0)),
            pl.BlockSpec((None, tn, k), lambda j, i: (layer, j, 0)),
            pl.BlockSpec((1, tn), lambda j, i: (0, j)),
        ],
        out_specs=pl.BlockSpec((tm, tn), lambda j, i: (i, j)),
        out_shape=jax.ShapeDtypeStruct((m, n_cols), bf16),
        scratch_shapes=[pltpu.VMEM((tn, k), bf16)],
        compiler_params=_params("parallel", "arbitrary"),
        name=name,
    )(x, w_stack, col_scale)


CONV_HALO = 8
CONV_ROWS = 256


def _in_proj_conv_kernel(x_ref, w_ref, cw_ref, cb_ref, o_ref, wb_ref, pad_ref, *, tiles_per_seq):
    i = pl.program_id(1)
    tm = x_ref.shape[0]

    @pl.when(i == 0)
    def _():
        wb_ref[...] = w_ref[...].astype(bf16)

    @pl.when(i % tiles_per_seq == 0)
    def _():
        pad_ref[0:CONV_HALO, :] = jnp.zeros((CONV_HALO, pad_ref.shape[1]), f32)

    pad_ref[CONV_HALO:CONV_HALO + tm, :] = _dot_nt(x_ref[...], wb_ref[...])
    for r in range(tm // CONV_ROWS):
        base = CONV_HALO + r * CONV_ROWS
        acc = cb_ref[...] + cw_ref[SSD_CONV_K - 1:SSD_CONV_K, :] * pad_ref[base:base + CONV_ROWS, :]
        for j in range(SSD_CONV_K - 1):
            shift = SSD_CONV_K - 1 - j
            acc = acc + cw_ref[j:j + 1, :] * pad_ref[base - shift:base - shift + CONV_ROWS, :]
        o_ref[r * CONV_ROWS:(r + 1) * CONV_ROWS, :] = (acc * _sigmoid(acc)).astype(o_ref.dtype)
    pad_ref[0:CONV_HALO, :] = pad_ref[tm:tm + CONV_HALO, :]


def _in_proj_conv(x, w_stack, conv_w, conv_b, layer, col0, n_cols, seq, tm=1024, tn=1024):
    m, k = x.shape
    ck = conv_w.shape[1]
    return pl.pallas_call(
        functools.partial(_in_proj_conv_kernel, tiles_per_seq=seq // tm),
        grid=(n_cols // tn, m // tm),
        in_specs=[
            pl.BlockSpec((tm, k), lambda j, i: (i, 0)),
            pl.BlockSpec((None, tn, k), lambda j, i: (layer, col0 // tn + j, 0)),
            pl.BlockSpec((None, ck, tn), lambda j, i: (layer, 0, j)),
            pl.BlockSpec((None, 1, tn), lambda j, i: (layer, 0, j)),
        ],
        out_specs=pl.BlockSpec((tm, tn), lambda j, i: (i, j)),
        out_shape=jax.ShapeDtypeStruct((m, n_cols), bf16),
        scratch_shapes=[pltpu.VMEM((tn, k), bf16), pltpu.VMEM((CONV_HALO + tm, tn), f32)],
        compiler_params=_params("parallel", "arbitrary"),
        name="ssd_in_proj_conv",
    )(x, w_stack, conv_w, conv_b.reshape(conv_b.shape[0], 1, n_cols))


def _ffn_up_kernel(x_ref, wg_ref, wu_ref, o_ref, wgb_ref, wub_ref):
    @pl.when(pl.program_id(1) == 0)
    def _():
        wgb_ref[...] = wg_ref[...].astype(bf16)
        wub_ref[...] = wu_ref[...].astype(bf16)

    x = x_ref[...]
    g = _dot(x, wgb_ref[...])
    u = _dot(x, wub_ref[...])
    o_ref[...] = (g * _sigmoid(g) * u).astype(o_ref.dtype)


def _ffn_up(x, wg_stack, wu_stack, layer, tm=1024, tn=512):
    m, k = x.shape
    n = wg_stack.shape[2]
    w_spec = pl.BlockSpec((None, k, tn), lambda j, i: (layer, 0, j))
    return pl.pallas_call(
        _ffn_up_kernel,
        grid=(n // tn, m // tm),
        in_specs=[pl.BlockSpec((tm, k), lambda j, i: (i, 0)), w_spec, w_spec],
        out_specs=pl.BlockSpec((tm, tn), lambda j, i: (i, j)),
        out_shape=jax.ShapeDtypeStruct((m, n), bf16),
        scratch_shapes=[pltpu.VMEM((k, tn), bf16), pltpu.VMEM((k, tn), bf16)],
        compiler_params=_params("parallel", "arbitrary"),
        name="ffn_up",
    )(x, wg_stack, wu_stack)


WEIGHT_CHUNK_ROWS = 256


def _load_weight_bf16(w_hbm, wb_ref, stage_ref, sem_ref):
    rows = WEIGHT_CHUNK_ROWS
    n_chunks = wb_ref.shape[0] // rows

    def chunk_copy(c):
        return pltpu.make_async_copy(w_hbm.at[pl.ds(c * rows, rows)], stage_ref.at[c % 2], sem_ref.at[c % 2])

    chunk_copy(0).start()
    for c in range(n_chunks):
        if c + 1 < n_chunks:
            chunk_copy(c + 1).start()
        chunk_copy(c).wait()
        wb_ref[c * rows:(c + 1) * rows, :] = stage_ref[c % 2].astype(bf16)


def _weight_scratch(k, n):
    return [pltpu.VMEM((k, n), bf16), pltpu.VMEM((2, WEIGHT_CHUNK_ROWS, n), f32), pltpu.SemaphoreType.DMA((2,))]


def _next_activation(hnew, gnext_ref, next_mode):
    if next_mode == "norm":
        return _rms(hnew, gnext_ref[...]).astype(bf16)
    return hnew.astype(bf16)


def _mm_res_kernel(x_ref, w_hbm, h_ref, gpost_ref, gnext_ref, hout_ref, hn_ref, wb_ref, stage_ref, sem_ref, *,
                   layer, next_mode):
    @pl.when(pl.program_id(0) == 0)
    def _():
        _load_weight_bf16(w_hbm.at[layer], wb_ref, stage_ref, sem_ref)

    mix = _dot(x_ref[...], wb_ref[...])
    hnew = h_ref[...] + _rms(mix, gpost_ref[...])
    hout_ref[...] = hnew
    hn_ref[...] = _next_activation(hnew, gnext_ref, next_mode)


def _ssd_out_kernel(y_ref, z_ref, normw_ref, w_hbm, h_ref, gpost_ref, gnext_ref, hout_ref, hn_ref,
                    wb_ref, stage_ref, sem_ref, *, layer):
    @pl.when(pl.program_id(0) == 0)
    def _():
        _load_weight_bf16(w_hbm.at[layer], wb_ref, stage_ref, sem_ref)

    gw = SSD_GROUP_WIDTH
    parts = []
    for g in range(SSD_N_GROUPS):
        cols = slice(g * gw, (g + 1) * gw)
        z = z_ref[:, cols].astype(f32)
        y = y_ref[:, cols].astype(f32) * (z * _sigmoid(z))
        parts.append(_rms(y, normw_ref[:, cols]).astype(bf16))
    mix = _dot(jnp.concatenate(parts, axis=1), wb_ref[...])
    hnew = h_ref[...] + _rms(mix, gpost_ref[...])
    hout_ref[...] = hnew
    hn_ref[...] = _next_activation(hnew, gnext_ref, "norm")


def _gain_spec(d, layer):
    return pl.BlockSpec((None, 1, d), lambda i: (layer, 0, 0))


def _gains3(g):
    return g.reshape(g.shape[0], 1, g.shape[1])


def _ssd_out(y, z, norm_w, w_stack, w_layer, h, gpost, gnext, layer, tm=256):
    m, k = y.shape
    d = h.shape[1]
    row = lambda i: (i, 0)
    return pl.pallas_call(
        functools.partial(_ssd_out_kernel, layer=w_layer),
        grid=(m // tm,),
        in_specs=[
            pl.BlockSpec((tm, k), row),
            pl.BlockSpec((tm, k), row),
            _gain_spec(k, w_layer),
            pl.BlockSpec(memory_space=pl.ANY),
            pl.BlockSpec((tm, d), row),
            _gain_spec(d, layer),
            _gain_spec(d, layer),
        ],
        out_specs=[pl.BlockSpec((tm, d), row), pl.BlockSpec((tm, d), row)],
        out_shape=[jax.ShapeDtypeStruct((m, d), f32), jax.ShapeDtypeStruct((m, d), bf16)],
        scratch_shapes=_weight_scratch(k, d),
        compiler_params=_params("arbitrary", vmem=VMEM_LIMIT_RESIDENT_BYTES),
        name="ssd_out_proj",
    )(y, z, _gains3(norm_w), w_stack, h, _gains3(gpost), _gains3(gnext))


def _matmul_residual(x, w_stack, w_layer, h, gpost, gnext, layer, next_mode, name, tm=256):
    m, k = x.shape
    d = h.shape[1]
    row = lambda i: (i, 0)
    return pl.pallas_call(
        functools.partial(_mm_res_kernel, layer=w_layer, next_mode=next_mode),
        grid=(m // tm,),
        in_specs=[
            pl.BlockSpec((tm, k), row),
            pl.BlockSpec(memory_space=pl.ANY),
            pl.BlockSpec((tm, d), row),
            _gain_spec(d, layer),
            _gain_spec(d, layer),
        ],
        out_specs=[pl.BlockSpec((tm, d), row), pl.BlockSpec((tm, d), row)],
        out_shape=[jax.ShapeDtypeStruct((m, d), f32), jax.ShapeDtypeStruct((m, d), bf16)],
        scratch_shapes=_weight_scratch(k, d),
        compiler_params=_params("arbitrary", vmem=VMEM_LIMIT_RESIDENT_BYTES),
        name=name,
    )(x, w_stack, h, _gains3(gpost), _gains3(gnext))


def _ple_kernel(hb_ref, h_ref, p_ref, wgate_hbm, wproj_ref, gple_ref, gnext_ref, *refs, layer, next_mode):
    n_out = 1 if next_mode == "none" else 2
    out_refs = refs[:n_out]
    wb_ref, stage_ref, sem_ref = refs[n_out:]

    @pl.when(pl.program_id(0) == 0)
    def _():
        _load_weight_bf16(wgate_hbm.at[layer], wb_ref, stage_ref, sem_ref)

    gate = _dot(hb_ref[...], wb_ref[...])
    pe = _rms(_dot(p_ref[...].astype(bf16), wproj_ref[...].astype(bf16)), gple_ref[...])
    hnew = h_ref[...] + _sigmoid(gate) * pe
    out_refs[0][...] = hnew
    if next_mode != "none":
        out_refs[1][...] = _next_activation(hnew, gnext_ref, next_mode)


def _ple(hb, h, p_stack, wgate_stack, wproj_stack, gple, gnext, layer, next_layer, next_mode, tm=512):
    m, d = h.shape
    pd = p_stack.shape[2]
    row = lambda i: (i, 0)
    out_specs = [pl.BlockSpec((tm, d), row)]
    out_shape = [jax.ShapeDtypeStruct((m, d), f32)]
    if next_mode != "none":
        out_specs.append(pl.BlockSpec((tm, d), row))
        out_shape.append(jax.ShapeDtypeStruct((m, d), bf16))
    return pl.pallas_call(
        functools.partial(_ple_kernel, layer=layer, next_mode=next_mode),
        grid=(m // tm,),
        in_specs=[
            pl.BlockSpec((tm, d), row),
            pl.BlockSpec((tm, d), row),
            pl.BlockSpec((None, tm, pd), lambda i: (layer, i, 0)),
            pl.BlockSpec(memory_space=pl.ANY),
            pl.BlockSpec((None, pd, d), lambda i: (layer, 0, 0), pipeline_mode=pl.Buffered(1)),
            _gain_spec(d, layer),
            _gain_spec(d, next_layer),
        ],
        out_specs=out_specs,
        out_shape=out_shape,
        scratch_shapes=_weight_scratch(d, d),
        compiler_params=_params("arbitrary", vmem=VMEM_LIMIT_RESIDENT_BYTES),
        name="ple",
    )(hb, h, p_stack, wgate_stack, wproj_stack, _gains3(gple), _gains3(gnext))


LOG2E = math.log2(math.e)


def _narrow_weight(w_ref, n_valid):
    row = lax.broadcasted_iota(jnp.int32, w_ref.shape, 0)
    return jnp.where(row < n_valid, w_ref[...], 0.0).astype(bf16)


def _ssd_dt_kernel(hn_ref, w_ref, bias_ref, alog_ref, dt_ref, acum_ref, acum_t_ref):
    hn = hn_ref[...]
    tm = hn.shape[0]
    q = SSD_CHUNK
    dt = _softplus(_dot_nt(hn, _narrow_weight(w_ref, SSD_N_HEADS)) + bias_ref[...])
    dt_ref[...] = dt
    adt = dt * (-jnp.exp(alog_ref[...]))
    tril = _tri(q, lower=True)
    for c in range(tm // q):
        hi, mid, lo = _split3(adt[c * q:(c + 1) * q, :])
        acum = (_dot(tril, hi) + _dot(tril, mid) + _dot(tril, lo)) * LOG2E
        acum_ref[c * q:(c + 1) * q, :] = acum
        acum_t_ref[:, c * q:(c + 1) * q] = acum.T[:SSD_N_HEADS, :]


def _ssd_dt(hn, w_stack, layer, dt_bias, a_log, tm=1024):
    m, d = hn.shape
    nh = SSD_N_HEADS
    pad_row = lambda v: jnp.pad(v, (0, LANES - nh)).reshape(1, LANES)
    fixed = lambda i: (0, 0)
    return pl.pallas_call(
        _ssd_dt_kernel,
        grid=(m // tm,),
        in_specs=[
            pl.BlockSpec((tm, d), lambda i: (i, 0)),
            pl.BlockSpec((None, LANES, d), lambda i: (layer, SSD_ZX_DIM // LANES, 0)),
            pl.BlockSpec((1, LANES), fixed),
            pl.BlockSpec((1, LANES), fixed),
        ],
        out_specs=[
            pl.BlockSpec((tm, LANES), lambda i: (i, 0)),
            pl.BlockSpec((tm, LANES), lambda i: (i, 0)),
            pl.BlockSpec((nh, tm), lambda i: (0, i)),
        ],
        out_shape=[
            jax.ShapeDtypeStruct((m, LANES), f32),
            jax.ShapeDtypeStruct((m, LANES), f32),
            jax.ShapeDtypeStruct((nh, m), f32),
        ],
        compiler_params=_params("parallel"),
        name="ssd_dt",
    )(hn, w_stack, pad_row(dt_bias), pad_row(a_log))


def _ssd_scan_kernel(xs_ref, b_ref, c_ref, dt_ref, acum_ref, acum_t_ref, dskip_ref, o_ref, state_ref):
    q = SSD_CHUNK
    gw = SSD_GROUP_WIDTH
    hd = SSD_HEAD_DIM

    @pl.when(pl.program_id(1) == 0)
    def _():
        state_ref[...] = jnp.zeros_like(state_ref)

    rows = lax.broadcasted_iota(jnp.int32, (q, q), 0)
    cols = lax.broadcasted_iota(jnp.int32, (q, q), 1)
    tril = rows >= cols
    first_head = cols < hd

    for g in range(SSD_N_GROUPS):
        bg = b_ref[:, g * SSD_D_STATE:(g + 1) * SSD_D_STATE]
        cg = c_ref[:, g * SSD_D_STATE:(g + 1) * SSD_D_STATE]
        cb = _dot_nt(cg, bg)
        xg = xs_ref[:, g * gw:(g + 1) * gw].astype(f32)

        acol = []
        dt_pairs = []
        ac_pairs = []
        for i in range(SSD_HEADS_PER_GROUP):
            h = g * SSD_HEADS_PER_GROUP + i
            acol.append(jnp.broadcast_to(acum_ref[:, h:h + 1], (q, q)))
        for pr in range(SSD_HEADS_PER_GROUP // 2):
            h0 = g * SSD_HEADS_PER_GROUP + 2 * pr
            d0 = jnp.broadcast_to(dt_ref[:, h0:h0 + 1], (q, q))
            d1 = jnp.broadcast_to(dt_ref[:, h0 + 1:h0 + 2], (q, q))
            dt_pairs.append(jnp.where(first_head, d0, d1))
            ac_pairs.append(jnp.where(first_head, acol[2 * pr], acol[2 * pr + 1]))
        dt_g = jnp.concatenate(dt_pairs, axis=1)
        ac_g = jnp.concatenate(ac_pairs, axis=1)
        a_last = ac_g[q - 1:q, :]

        xdt = xg * dt_g
        st_new = _dot_tn(bg, (xdt * jnp.exp2(a_last - ac_g)).astype(bf16))
        prev = state_ref[g]
        y = _dot(cg, prev.astype(bf16)) * jnp.exp2(ac_g)
        state_ref[g] = prev * jnp.exp2(a_last) + st_new

        y_pairs = []
        for pr in range(SSD_HEADS_PER_GROUP // 2):
            xp = xdt[:, pr * q:(pr + 1) * q]
            rhs = jnp.concatenate(
                [jnp.where(first_head, xp, 0.0).astype(bf16), jnp.where(first_head, 0.0, xp).astype(bf16)], axis=0)
            lhs = []
            for k in range(2):
                i = 2 * pr + k
                h = g * SSD_HEADS_PER_GROUP + i
                seg = acol[i] - acum_t_ref[h:h + 1, :]
                decay = jnp.exp2(jnp.where(tril, seg, -jnp.inf))
                lhs.append((cb * decay).astype(bf16))
            y_pairs.append(_dot(jnp.concatenate(lhs, axis=1), rhs))
        y = y + jnp.concatenate(y_pairs, axis=1) + xg * dskip_ref[:, g * gw:(g + 1) * gw]
        o_ref[:, g * gw:(g + 1) * gw] = y.astype(o_ref.dtype)


def _ssd_scan(xbc, dt, acum, acum_t, d_skip, batch, seq):
    m = xbc.shape[0]
    q = SSD_CHUNK
    nc = seq // q
    di = SSD_D_INNER
    bw = SSD_BC_WIDTH
    chunk = lambda b, c: (b * nc + c, 0)
    fixed = lambda b, c: (0, 0)
    d_cols = jnp.repeat(d_skip, SSD_HEAD_DIM).reshape(1, di)
    return pl.pallas_call(
        _ssd_scan_kernel,
        grid=(batch, nc),
        in_specs=[
            pl.BlockSpec((q, di), chunk),
            pl.BlockSpec((q, bw), lambda b, c: (b * nc + c, di // bw)),
            pl.BlockSpec((q, bw), lambda b, c: (b * nc + c, di // bw + 1)),
            pl.BlockSpec((q, LANES), chunk),
            pl.BlockSpec((q, LANES), chunk),
            pl.BlockSpec((SSD_N_HEADS, q), lambda b, c: (0, b * nc + c)),
            pl.BlockSpec((1, di), fixed),
        ],
        out_specs=pl.BlockSpec((q, di), chunk),
        out_shape=jax.ShapeDtypeStruct((m, di), bf16),
        scratch_shapes=[pltpu.VMEM((SSD_N_GROUPS, SSD_D_STATE, SSD_GROUP_WIDTH), f32)],
        compiler_params=_params("parallel", "arbitrary"),
        name="ssd_scan",
    )(xbc, xbc, xbc, dt, acum, acum_t, d_cols)


def _fox_gate_kernel(hn_ref, w_ref, b_row_ref, csum_ref):
    hn = hn_ref[...]
    seq = hn.shape[0]
    q = LANES
    logf = -_softplus(-(_dot_nt(hn, _narrow_weight(w_ref, FOX_N_HEADS)) + b_row_ref[...]))
    tril = _tri(q, lower=True)
    carry = jnp.zeros((1, LANES), f32)
    for c in range(seq // q):
        hi, mid, lo = _split3(logf[c * q:(c + 1) * q, :])
        cs = _dot(tril, hi) + _dot(tril, mid) + _dot(tril, lo) + carry
        csum_ref[c * q:(c + 1) * q, :] = cs * LOG2E
        carry = cs[q - 1:q, :]


def _fox_gate(hn, w_stack, layer, b_f, batch, seq):
    m, d = hn.shape
    nh = FOX_N_HEADS
    return pl.pallas_call(
        _fox_gate_kernel,
        grid=(batch,),
        in_specs=[
            pl.BlockSpec((seq, d), lambda b: (b, 0)),
            pl.BlockSpec((None, LANES, d), lambda b: (layer, 3 * FOX_WIDTH // LANES, 0)),
            pl.BlockSpec((1, LANES), lambda b: (0, 0)),
        ],
        out_specs=pl.BlockSpec((seq, LANES), lambda b: (b, 0)),
        out_shape=jax.ShapeDtypeStruct((m, LANES), f32),
        compiler_params=_params("parallel"),
        name="fox_gate",
    )(hn, w_stack, jnp.pad(b_f, (0, LANES - nh)).reshape(1, LANES))


ATTN_BLOCK = 512


def _fox_attn_kernel(q_ref, k_ref, v_ref, csum_ref, o_ref, qa_ref, ka_ref):
    seq = q_ref.shape[0]
    t = ATTN_BLOCK
    hd = FOX_HEAD_DIM
    head = pl.program_id(1)
    lane = lax.broadcasted_iota(jnp.int32, (seq, LANES), 1)
    c = jnp.sum(jnp.where(lane == head, csum_ref[...], 0.0), axis=1, keepdims=True)
    hi = c.astype(bf16).astype(f32)
    r1 = c - hi
    mid = r1.astype(bf16).astype(f32)
    lo = r1 - mid
    ext_q = jnp.where(lane == 0, hi, jnp.where(lane == 1, mid, jnp.where(lane == 2, lo,
                                                                         jnp.where(lane < 6, 1.0, 0.0))))
    ext_k = jnp.where(lane < 3, 1.0, jnp.where(lane == 3, -hi, jnp.where(lane == 4, -mid,
                                                                         jnp.where(lane == 5, -lo, 0.0))))
    qa_ref[:, :hd] = q_ref[...]
    qa_ref[:, hd:] = ext_q.astype(bf16)
    ka_ref[:, :hd] = k_ref[...]
    ka_ref[:, hd:] = ext_k.astype(bf16)

    causal = lax.broadcasted_iota(jnp.int32, (t, t), 0) >= lax.broadcasted_iota(jnp.int32, (t, t), 1)
    for qi in range(seq // t):
        q = qa_ref[qi * t:(qi + 1) * t, :]
        for j in range(qi + 1):
            s = _dot_nt(q, ka_ref[j * t:(j + 1) * t, :])
            if j == qi:
                s = jnp.where(causal, s, -jnp.inf)
            v = v_ref[j * t:(j + 1) * t, :]
            m_blk = jnp.max(s, axis=1, keepdims=True)
            if j == 0:
                m_run = m_blk
                p = jnp.exp2(s - m_run)
                l_run = jnp.sum(p, axis=1, keepdims=True)
                acc = _dot(p.astype(bf16), v)
            else:
                m_new = jnp.maximum(m_run, m_blk)
                alpha = jnp.exp2(m_run - m_new)
                p = jnp.exp2(s - m_new)
                l_run = alpha * l_run + jnp.sum(p, axis=1, keepdims=True)
                acc = alpha * acc + _dot(p.astype(bf16), v)
                m_run = m_new
        o_ref[qi * t:(qi + 1) * t, :] = (acc * (1.0 / l_run)).astype(o_ref.dtype)


def _fox_attention(qkv, csum, batch, seq):
    m = qkv.shape[0]
    nh = FOX_N_HEADS
    hd = FOX_HEAD_DIM
    return pl.pallas_call(
        _fox_attn_kernel,
        grid=(batch, nh),
        in_specs=[
            pl.BlockSpec((seq, hd), lambda b, h: (b, h)),
            pl.BlockSpec((seq, hd), lambda b, h: (b, nh + h)),
            pl.BlockSpec((seq, hd), lambda b, h: (b, 2 * nh + h)),
            pl.BlockSpec((seq, LANES), lambda b, h: (b, 0)),
        ],
        out_specs=pl.BlockSpec((seq, hd), lambda b, h: (b, h)),
        out_shape=jax.ShapeDtypeStruct((m, FOX_WIDTH), bf16),
        scratch_shapes=[pltpu.VMEM((seq, 2 * hd), bf16), pltpu.VMEM((seq, 2 * hd), bf16)],
        compiler_params=_params("parallel", "parallel"),
        name="fox_attn",
    )(qkv, qkv, qkv, csum)


def kernel(x, p, norm_mix_pre, norm_mix_post, norm_ffn_pre, norm_ffn_post, ssd_w_in, ssd_conv_w, ssd_conv_b, ssd_dt_bias, ssd_a_log, ssd_d, ssd_norm_w, ssd_w_out, fox_w_in, fox_b_f, fox_w_out, ffn_w_gate, ffn_w_up, ffn_w_down, ple_w_proj, ple_norm, ple_w_gate):
    batch, seq, d = x.shape
    depth = p.shape[0]
    m = batch * seq
    h = x.reshape(m, d)
    p_stack = p.reshape(depth, m, PLE_DIM)
    ssd_scale = jnp.ones((1, SSD_D_INNER), f32)
    fox_scale = jnp.concatenate(
        [jnp.full((1, FOX_WIDTH), FOX_HEAD_DIM ** -0.5 * LOG2E, f32), jnp.ones((1, 2 * FOX_WIDTH), f32)], axis=1)

    ssd_w_in = jnp.swapaxes(ssd_w_in, 1, 2)
    fox_w_in = jnp.swapaxes(fox_w_in, 1, 2)

    hn = _rmsnorm_bf16(h, norm_mix_pre, 0)
    for i in range(depth):
        j = i // 2
        if i % 2 == 0:
            z = _in_proj(hn, ssd_w_in, j, SSD_D_INNER, ssd_scale, "ssd_in_proj_z")
            xbc = _in_proj_conv(hn, ssd_w_in, ssd_conv_w, ssd_conv_b, j, SSD_D_INNER, SSD_CONV_DIM, seq)
            dt, acum, acum_t = _ssd_dt(hn, ssd_w_in, j, ssd_dt_bias[j], ssd_a_log[j])
            y = _ssd_scan(xbc, dt, acum, acum_t, ssd_d[j], batch, seq)
            h, hn = _ssd_out(y, z, ssd_norm_w, ssd_w_out, j, h, norm_mix_post, norm_ffn_pre, i)
        else:
            qkv = _in_proj(hn, fox_w_in, j, 3 * FOX_WIDTH, fox_scale, "fox_in_proj")
            csum = _fox_gate(hn, fox_w_in, j, fox_b_f[j], batch, seq)
            mixed = _fox_attention(qkv, csum, batch, seq)
            h, hn = _matmul_residual(mixed, fox_w_out, j, h, norm_mix_post, norm_ffn_pre, i, "norm", "fox_out_proj",
                                     tm=512)
        act = _ffn_up(hn, ffn_w_gate, ffn_w_up, i)
        h, hb = _matmul_residual(act, ffn_w_down, i, h, norm_ffn_post, norm_ffn_post, i, "cast", "ffn_down")
        last = i == depth - 1
        outs = _ple(hb, h, p_stack, ple_w_gate, ple_w_proj, ple_norm, norm_mix_pre, i, 0 if last else i + 1,
                    "none" if last else "norm")
        h = outs[0]
        if not last:
            hn = outs[1]
    return h.reshape(batch, seq, d)
```

```python
import functools
import math

import jax
import jax.numpy as jnp
from jax import lax
from jax.experimental import pallas as pl
from jax.experimental.pallas import tpu as pltpu

f32 = jnp.float32
bf16 = jnp.bfloat16

D_MODEL = 2048
EPS = 1e-6

SSD_D_INNER = 4096
SSD_HEAD_DIM = 64
SSD_N_HEADS = 64
SSD_N_GROUPS = 8
SSD_HEADS_PER_GROUP = 8
SSD_D_STATE = 128
SSD_CONV_K = 4
SSD_CHUNK = 128
SSD_GROUP_WIDTH = SSD_HEADS_PER_GROUP * SSD_HEAD_DIM
SSD_BC_WIDTH = SSD_N_GROUPS * SSD_D_STATE
SSD_CONV_DIM = SSD_D_INNER + 2 * SSD_BC_WIDTH
SSD_ZX_DIM = SSD_D_INNER + SSD_CONV_DIM

FOX_N_HEADS = 16
FOX_HEAD_DIM = 128
FOX_WIDTH = 2048

FFN_HIDDEN = 5632
PLE_DIM = 256

LANES = 128
VMEM_LIMIT_BYTES = 48 * 1024 * 1024
VMEM_LIMIT_RESIDENT_BYTES = 58 * 1024 * 1024


def _params(*semantics, vmem=VMEM_LIMIT_BYTES):
    return pltpu.CompilerParams(dimension_semantics=semantics, vmem_limit_bytes=vmem)


def _dot(a, b):
    return jnp.dot(a, b, preferred_element_type=f32)


def _dot_nt(a, b):
    return lax.dot_general(a, b, (((1,), (1,)), ((), ())), preferred_element_type=f32)


def _dot_tn(a, b):
    return lax.dot_general(a, b, (((0,), (0,)), ((), ())), preferred_element_type=f32)


def _rms(x, gain):
    return x * lax.rsqrt(jnp.mean(x * x, axis=-1, keepdims=True) + EPS) * gain


def _sigmoid(x):
    return 1.0 / (1.0 + jnp.exp(-x))


def _softplus(x):
    return jnp.maximum(x, 0.0) + jnp.log1p(jnp.exp(-jnp.abs(x)))


def _split3(x):
    hi = x.astype(bf16)
    r1 = x - hi.astype(f32)
    mid = r1.astype(bf16)
    lo = (r1 - mid.astype(f32)).astype(bf16)
    return hi, mid, lo


def _tri(n, lower):
    r = lax.broadcasted_iota(jnp.int32, (n, n), 0)
    c = lax.broadcasted_iota(jnp.int32, (n, n), 1)
    return jnp.where((r >= c) if lower else (r <= c), 1.0, 0.0).astype(bf16)


def _rmsnorm_kernel(x_ref, g_ref, o_ref):
    o_ref[...] = _rms(x_ref[...], g_ref[...]).astype(o_ref.dtype)


def _rmsnorm_bf16(x, gains, layer, tm=512):
    m, d = x.shape
    return pl.pallas_call(
        _rmsnorm_kernel,
        grid=(m // tm,),
        in_specs=[pl.BlockSpec((tm, d), lambda i: (i, 0)), pl.BlockSpec((None, 1, d), lambda i: (layer, 0, 0))],
        out_specs=pl.BlockSpec((tm, d), lambda i: (i, 0)),
        out_shape=jax.ShapeDtypeStruct((m, d), bf16),
        compiler_params=_params("parallel"),
        name="rmsnorm",
    )(x, gains.reshape(gains.shape[0], 1, d))


def _in_proj_kernel(x_ref, w_ref, scale_ref, o_ref, wb_ref):
    @pl.when(pl.program_id(1) == 0)
    def _():
        wb_ref[...] = w_ref[...].astype(bf16)

    o_ref[...] = (_dot_nt(x_ref[...], wb_ref[...]) * scale_ref[...]).astype(o_ref.dtype)


def _in_proj(x, w_stack, layer, n_cols, col_scale, name, tm=2048, tn=1024):
    m, k = x.shape
    return pl.pallas_call(
        _in_proj_kernel,
        grid=(n_cols // tn, m // tm),
        in_specs=[
            pl.BlockSpec((tm, k), lambda j, i: (i, 0)),
            pl.BlockSpec((None, tn, k), lambda j, i: (layer, j, 0)),
            pl.BlockSpec((1, tn), lambda j, i: (0, j)),
        ],
        out_specs=pl.BlockSpec((tm, tn), lambda j, i: (i, j)),
        out_shape=jax.ShapeDtypeStruct((m, n_cols), bf16),
        scratch_shapes=[pltpu.VMEM((tn, k), bf16)],
        compiler_params=_params("parallel", "arbitrary", vmem=VMEM_LIMIT_RESIDENT_BYTES),
        name=name,
    )(x, w_stack, col_scale)


CONV_HALO = 8
CONV_ROWS = 256


def _in_proj_conv_kernel(x_ref, w_ref, cw_ref, cb_ref, o_ref, wb_ref, pad_ref, *, tiles_per_seq):
    i = pl.program_id(1)
    tm = x_ref.shape[0]

    @pl.when(i == 0)
    def _():
        wb_ref[...] = w_ref[...].astype(bf16)

    @pl.when(i % tiles_per_seq == 0)
    def _():
        pad_ref[0:CONV_HALO, :] = jnp.zeros((CONV_HALO, pad_ref.shape[1]), f32)

    pad_ref[CONV_HALO:CONV_HALO + tm, :] = _dot_nt(x_ref[...], wb_ref[...])
    for r in range(tm // CONV_ROWS):
        base = CONV_HALO + r * CONV_ROWS
        acc = cb_ref[...] + cw_ref[SSD_CONV_K - 1:SSD_CONV_K, :] * pad_ref[base:base + CONV_ROWS, :]
        for j in range(SSD_CONV_K - 1):
            shift = SSD_CONV_K - 1 - j
            acc = acc + cw_ref[j:j + 1, :] * pad_ref[base - shift:base - shift + CONV_ROWS, :]
        o_ref[r * CONV_ROWS:(r + 1) * CONV_ROWS, :] = (acc * _sigmoid(acc)).astype(o_ref.dtype)
    pad_ref[0:CONV_HALO, :] = pad_ref[tm:tm + CONV_HALO, :]


def _in_proj_conv(x, w_stack, conv_w, conv_b, layer, col0, n_cols, seq, tm=1024, tn=1024):
    m, k = x.shape
    ck = conv_w.shape[1]
    return pl.pallas_call(
        functools.partial(_in_proj_conv_kernel, tiles_per_seq=seq // tm),
        grid=(n_cols // tn, m // tm),
        in_specs=[
            pl.BlockSpec((tm, k), lambda j, i: (i, 0)),
            pl.BlockSpec((None, tn, k), lambda j, i: (layer, col0 // tn + j, 0)),
            pl.BlockSpec((None, ck, tn), lambda j, i: (layer, 0, j)),
            pl.BlockSpec((None, 1, tn), lambda j, i: (layer, 0, j)),
        ],
        out_specs=pl.BlockSpec((tm, tn), lambda j, i: (i, j)),
        out_shape=jax.ShapeDtypeStruct((m, n_cols), bf16),
        scratch_shapes=[pltpu.VMEM((tn, k), bf16), pltpu.VMEM((CONV_HALO + tm, tn), f32)],
        compiler_params=_params("parallel", "arbitrary"),
        name="ssd_in_proj_conv",
    )(x, w_stack, conv_w, conv_b.reshape(conv_b.shape[0], 1, n_cols))


def _ffn_up_kernel(x_ref, wg_ref, wu_ref, o_ref, wgb_ref, wub_ref):
    @pl.when(pl.program_id(1) == 0)
    def _():
        wgb_ref[...] = wg_ref[...].astype(bf16)
        wub_ref[...] = wu_ref[...].astype(bf16)

    x = x_ref[...]
    g = _dot(x, wgb_ref[...])
    u = _dot(x, wub_ref[...])
    o_ref[...] = (g * _sigmoid(g) * u).astype(o_ref.dtype)


def _ffn_up(x, wg_stack, wu_stack, layer, tm=2048, tn=512):
    m, k = x.shape
    n = wg_stack.shape[2]
    w_spec = pl.BlockSpec((None, k, tn), lambda j, i: (layer, 0, j))
    return pl.pallas_call(
        _ffn_up_kernel,
        grid=(n // tn, m // tm),
        in_specs=[pl.BlockSpec((tm, k), lambda j, i: (i, 0)), w_spec, w_spec],
        out_specs=pl.BlockSpec((tm, tn), lambda j, i: (i, j)),
        out_shape=jax.ShapeDtypeStruct((m, n), bf16),
        scratch_shapes=[pltpu.VMEM((k, tn), bf16), pltpu.VMEM((k, tn), bf16)],
        compiler_params=_params("parallel", "arbitrary", vmem=VMEM_LIMIT_RESIDENT_BYTES),
        name="ffn_up",
    )(x, wg_stack, wu_stack)


WEIGHT_CHUNK_ROWS = 256


def _load_weight_bf16(w_hbm, wb_ref, stage_ref, sem_ref):
    rows = WEIGHT_CHUNK_ROWS
    n_chunks = wb_ref.shape[0] // rows

    def chunk_copy(c):
        return pltpu.make_async_copy(w_hbm.at[pl.ds(c * rows, rows)], stage_ref.at[c % 2], sem_ref.at[c % 2])

    chunk_copy(0).start()
    for c in range(n_chunks):
        if c + 1 < n_chunks:
            chunk_copy(c + 1).start()
        chunk_copy(c).wait()
        wb_ref[c * rows:(c + 1) * rows, :] = stage_ref[c % 2].astype(bf16)


def _weight_scratch(k, n):
    return [pltpu.VMEM((k, n), bf16), pltpu.VMEM((2, WEIGHT_CHUNK_ROWS, n), f32), pltpu.SemaphoreType.DMA((2,))]


def _next_activation(hnew, gnext_ref, next_mode):
    if next_mode == "norm":
        return _rms(hnew, gnext_ref[...]).astype(bf16)
    return hnew.astype(bf16)


def _mm_res_kernel(x_ref, w_hbm, h_ref, gpost_ref, gnext_ref, hout_ref, hn_ref, wb_ref, stage_ref, sem_ref, *,
                   layer, next_mode):
    @pl.when(pl.program_id(0) == 0)
    def _():
        _load_weight_bf16(w_hbm.at[layer], wb_ref, stage_ref, sem_ref)

    mix = _dot(x_ref[...], wb_ref[...])
    hnew = h_ref[...] + _rms(mix, gpost_ref[...])
    hout_ref[...] = hnew
    hn_ref[...] = _next_activation(hnew, gnext_ref, next_mode)


def _ssd_out_kernel(y_ref, z_ref, normw_ref, w_hbm, h_ref, gpost_ref, gnext_ref, hout_ref, hn_ref,
                    wb_ref, stage_ref, sem_ref, *, layer):
    @pl.when(pl.program_id(0) == 0)
    def _():
        _load_weight_bf16(w_hbm.at[layer], wb_ref, stage_ref, sem_ref)

    gw = SSD_GROUP_WIDTH
    parts = []
    for g in range(SSD_N_GROUPS):
        cols = slice(g * gw, (g + 1) * gw)
        z = z_ref[:, cols].astype(f32)
        y = y_ref[:, cols].astype(f32) * (z * _sigmoid(z))
        parts.append(_rms(y, normw_ref[:, cols]).astype(bf16))
    mix = _dot(jnp.concatenate(parts, axis=1), wb_ref[...])
    hnew = h_ref[...] + _rms(mix, gpost_ref[...])
    hout_ref[...] = hnew
    hn_ref[...] = _next_activation(hnew, gnext_ref, "norm")


def _gain_spec(d, layer):
    return pl.BlockSpec((None, 1, d), lambda i: (layer, 0, 0))


def _gains3(g):
    return g.reshape(g.shape[0], 1, g.shape[1])


def _ssd_out(y, z, norm_w, w_stack, w_layer, h, gpost, gnext, layer, tm=256):
    m, k = y.shape
    d = h.shape[1]
    row = lambda i: (i, 0)
    return pl.pallas_call(
        functools.partial(_ssd_out_kernel, layer=w_layer),
        grid=(m // tm,),
        in_specs=[
            pl.BlockSpec((tm, k), row),
            pl.BlockSpec((tm, k), row),
            _gain_spec(k, w_layer),
            pl.BlockSpec(memory_space=pl.ANY),
            pl.BlockSpec((tm, d), row),
            _gain_spec(d, layer),
            _gain_spec(d, layer),
        ],
        out_specs=[pl.BlockSpec((tm, d), row), pl.BlockSpec((tm, d), row)],
        out_shape=[jax.ShapeDtypeStruct((m, d), f32), jax.ShapeDtypeStruct((m, d), bf16)],
        scratch_shapes=_weight_scratch(k, d),
        compiler_params=_params("arbitrary", vmem=VMEM_LIMIT_RESIDENT_BYTES),
        name="ssd_out_proj",
    )(y, z, _gains3(norm_w), w_stack, h, _gains3(gpost), _gains3(gnext))


def _matmul_residual(x, w_stack, w_layer, h, gpost, gnext, layer, next_mode, name, tm=256):
    m, k = x.shape
    d = h.shape[1]
    row = lambda i: (i, 0)
    return pl.pallas_call(
        functools.partial(_mm_res_kernel, layer=w_layer, next_mode=next_mode),
        grid=(m // tm,),
        in_specs=[
            pl.BlockSpec((tm, k), row),
            pl.BlockSpec(memory_space=pl.ANY),
            pl.BlockSpec((tm, d), row),
            _gain_spec(d, layer),
            _gain_spec(d, layer),
        ],
        out_specs=[pl.BlockSpec((tm, d), row), pl.BlockSpec((tm, d), row)],
        out_shape=[jax.ShapeDtypeStruct((m, d), f32), jax.ShapeDtypeStruct((m, d), bf16)],
        scratch_shapes=_weight_scratch(k, d),
        compiler_params=_params("arbitrary", vmem=VMEM_LIMIT_RESIDENT_BYTES),
        name=name,
    )(x, w_stack, h, _gains3(gpost), _gains3(gnext))


def _ple_kernel(hb_ref, h_ref, p_ref, wgate_hbm, wproj_ref, gple_ref, gnext_ref, *refs, layer, next_mode):
    n_out = 1 if next_mode == "none" else 2
    out_refs = refs[:n_out]
    wb_ref, stage_ref, sem_ref = refs[n_out:]

    @pl.when(pl.program_id(0) == 0)
    def _():
        _load_weight_bf16(wgate_hbm.at[layer], wb_ref, stage_ref, sem_ref)

    gate = _dot(hb_ref[...], wb_ref[...])
    pe = _rms(_dot(p_ref[...].astype(bf16), wproj_ref[...].astype(bf16)), gple_ref[...])
    hnew = h_ref[...] + _sigmoid(gate) * pe
    out_refs[0][...] = hnew
    if next_mode != "none":
        out_refs[1][...] = _next_activation(hnew, gnext_ref, next_mode)


def _ple(hb, h, p_stack, wgate_stack, wproj_stack, gple, gnext, layer, next_layer, next_mode, tm=512):
    m, d = h.shape
    pd = p_stack.shape[2]
    row = lambda i: (i, 0)
    out_specs = [pl.BlockSpec((tm, d), row)]
    out_shape = [jax.ShapeDtypeStruct((m, d), f32)]
    if next_mode != "none":
        out_specs.append(pl.BlockSpec((tm, d), row))
        out_shape.append(jax.ShapeDtypeStruct((m, d), bf16))
    return pl.pallas_call(
        functools.partial(_ple_kernel, layer=layer, next_mode=next_mode),
        grid=(m // tm,),
        in_specs=[
            pl.BlockSpec((tm, d), row),
            pl.BlockSpec((tm, d), row),
            pl.BlockSpec((None, tm, pd), lambda i: (layer, i, 0)),
            pl.BlockSpec(memory_space=pl.ANY),
            pl.BlockSpec((None, pd, d), lambda i: (layer, 0, 0), pipeline_mode=pl.Buffered(1)),
            _gain_spec(d, layer),
            _gain_spec(d, next_layer),
        ],
        out_specs=out_specs,
        out_shape=out_shape,
        scratch_shapes=_weight_scratch(d, d),
        compiler_params=_params("arbitrary", vmem=VMEM_LIMIT_RESIDENT_BYTES),
        name="ple",
    )(hb, h, p_stack, wgate_stack, wproj_stack, _gains3(gple), _gains3(gnext))


LOG2E = math.log2(math.e)


def _narrow_weight(w_ref, n_valid):
    row = lax.broadcasted_iota(jnp.int32, w_ref.shape, 0)
    return jnp.where(row < n_valid, w_ref[...], 0.0).astype(bf16)


def _ssd_dt_kernel(hn_ref, w_ref, bias_ref, alog_ref, dt_ref, acum_ref, acum_t_ref):
    hn = hn_ref[...]
    tm = hn.shape[0]
    q = SSD_CHUNK
    dt = _softplus(_dot_nt(hn, _narrow_weight(w_ref, SSD_N_HEADS)) + bias_ref[...])
    dt_ref[...] = dt
    adt = dt * (-jnp.exp(alog_ref[...]))
    tril = _tri(q, lower=True)
    for c in range(tm // q):
        hi, mid, lo = _split3(adt[c * q:(c + 1) * q, :])
        acum = (_dot(tril, hi) + _dot(tril, mid) + _dot(tril, lo)) * LOG2E
        acum_ref[c * q:(c + 1) * q, :] = acum
        acum_t_ref[:, c * q:(c + 1) * q] = acum.T[:SSD_N_HEADS, :]


def _ssd_dt(hn, w_stack, layer, dt_bias, a_log, tm=1024):
    m, d = hn.shape
    nh = SSD_N_HEADS
    pad_row = lambda v: jnp.pad(v, (0, LANES - nh)).reshape(1, LANES)
    fixed = lambda i: (0, 0)
    return pl.pallas_call(
        _ssd_dt_kernel,
        grid=(m // tm,),
        in_specs=[
            pl.BlockSpec((tm, d), lambda i: (i, 0)),
            pl.BlockSpec((None, LANES, d), lambda i: (layer, SSD_ZX_DIM // LANES, 0)),
            pl.BlockSpec((1, LANES), fixed),
            pl.BlockSpec((1, LANES), fixed),
        ],
        out_specs=[
            pl.BlockSpec((tm, LANES), lambda i: (i, 0)),
            pl.BlockSpec((tm, LANES), lambda i: (i, 0)),
            pl.BlockSpec((nh, tm), lambda i: (0, i)),
        ],
        out_shape=[
            jax.ShapeDtypeStruct((m, LANES), f32),
            jax.ShapeDtypeStruct((m, LANES), f32),
            jax.ShapeDtypeStruct((nh, m), f32),
        ],
        compiler_params=_params("parallel"),
        name="ssd_dt",
    )(hn, w_stack, pad_row(dt_bias), pad_row(a_log))


def _ssd_scan_kernel(xs_ref, b_ref, c_ref, dt_ref, acum_ref, acum_t_ref, dskip_ref, o_ref, state_ref):
    q = SSD_CHUNK
    gw = SSD_GROUP_WIDTH
    hd = SSD_HEAD_DIM

    @pl.when(pl.program_id(1) == 0)
    def _():
        state_ref[...] = jnp.zeros_like(state_ref)

    rows = lax.broadcasted_iota(jnp.int32, (q, q), 0)
    cols = lax.broadcasted_iota(jnp.int32, (q, q), 1)
    tril = rows >= cols
    first_head = cols < hd

    for g in range(SSD_N_GROUPS):
        bg = b_ref[:, g * SSD_D_STATE:(g + 1) * SSD_D_STATE]
        cg = c_ref[:, g * SSD_D_STATE:(g + 1) * SSD_D_STATE]
        cb = _dot_nt(cg, bg)
        xg = xs_ref[:, g * gw:(g + 1) * gw].astype(f32)

        acol = []
        dt_pairs = []
        ac_pairs = []
        for i in range(SSD_HEADS_PER_GROUP):
            h = g * SSD_HEADS_PER_GROUP + i
            acol.append(jnp.broadcast_to(acum_ref[:, h:h + 1], (q, q)))
        for pr in range(SSD_HEADS_PER_GROUP // 2):
            h0 = g * SSD_HEADS_PER_GROUP + 2 * pr
            d0 = jnp.broadcast_to(dt_ref[:, h0:h0 + 1], (q, q))
            d1 = jnp.broadcast_to(dt_ref[:, h0 + 1:h0 + 2], (q, q))
            dt_pairs.append(jnp.where(first_head, d0, d1))
            ac_pairs.append(jnp.where(first_head, acol[2 * pr], acol[2 * pr + 1]))
        dt_g = jnp.concatenate(dt_pairs, axis=1)
        ac_g = jnp.concatenate(ac_pairs, axis=1)
        a_last = ac_g[q - 1:q, :]

        xdt = xg * dt_g
        st_new = _dot_tn(bg, (xdt * jnp.exp2(a_last - ac_g)).astype(bf16))
        prev = state_ref[g]
        y = _dot(cg, prev.astype(bf16)) * jnp.exp2(ac_g)
        state_ref[g] = prev * jnp.exp2(a_last) + st_new

        y_pairs = []
        for pr in range(SSD_HEADS_PER_GROUP // 2):
            xp = xdt[:, pr * q:(pr + 1) * q]
            rhs = jnp.concatenate(
                [jnp.where(first_head, xp, 0.0).astype(bf16), jnp.where(first_head, 0.0, xp).astype(bf16)], axis=0)
            lhs = []
            for k in range(2):
                i = 2 * pr + k
                h = g * SSD_HEADS_PER_GROUP + i
                seg = acol[i] - acum_t_ref[h:h + 1, :]
                decay = jnp.exp2(jnp.where(tril, seg, -jnp.inf))
                lhs.append((cb * decay).astype(bf16))
            y_pairs.append(_dot(jnp.concatenate(lhs, axis=1), rhs))
        y = y + jnp.concatenate(y_pairs, axis=1) + xg * dskip_ref[:, g * gw:(g + 1) * gw]
        o_ref[:, g * gw:(g + 1) * gw] = y.astype(o_ref.dtype)


def _ssd_scan(xbc, dt, acum, acum_t, d_skip, batch, seq):
    m = xbc.shape[0]
    q = SSD_CHUNK
    nc = seq // q
    di = SSD_D_INNER
    bw = SSD_BC_WIDTH
    chunk = lambda b, c: (b * nc + c, 0)
    fixed = lambda b, c: (0, 0)
    d_cols = jnp.repeat(d_skip, SSD_HEAD_DIM).reshape(1, di)
    return pl.pallas_call(
        _ssd_scan_kernel,
        grid=(batch, nc),
        in_specs=[
            pl.BlockSpec((q, di), chunk),
            pl.BlockSpec((q, bw), lambda b, c: (b * nc + c, di // bw)),
            pl.BlockSpec((q, bw), lambda b, c: (b * nc + c, di // bw + 1)),
            pl.BlockSpec((q, LANES), chunk),
            pl.BlockSpec((q, LANES), chunk),
            pl.BlockSpec((SSD_N_HEADS, q), lambda b, c: (0, b * nc + c)),
            pl.BlockSpec((1, di), fixed),
        ],
        out_specs=pl.BlockSpec((q, di), chunk),
        out_shape=jax.ShapeDtypeStruct((m, di), bf16),
        scratch_shapes=[pltpu.VMEM((SSD_N_GROUPS, SSD_D_STATE, SSD_GROUP_WIDTH), f32)],
        compiler_params=_params("parallel", "arbitrary"),
        name="ssd_scan",
    )(xbc, xbc, xbc, dt, acum, acum_t, d_cols)


def _fox_gate_kernel(hn_ref, w_ref, b_row_ref, csum_ref):
    hn = hn_ref[...]
    seq = hn.shape[0]
    q = LANES
    logf = -_softplus(-(_dot_nt(hn, _narrow_weight(w_ref, FOX_N_HEADS)) + b_row_ref[...]))
    tril = _tri(q, lower=True)
    carry = jnp.zeros((1, LANES), f32)
    for c in range(seq // q):
        hi, mid, lo = _split3(logf[c * q:(c + 1) * q, :])
        cs = _dot(tril, hi) + _dot(tril, mid) + _dot(tril, lo) + carry
        csum_ref[c * q:(c + 1) * q, :] = cs * LOG2E
        carry = cs[q - 1:q, :]


def _fox_gate(hn, w_stack, layer, b_f, batch, seq):
    m, d = hn.shape
    nh = FOX_N_HEADS
    return pl.pallas_call(
        _fox_gate_kernel,
        grid=(batch,),
        in_specs=[
            pl.BlockSpec((seq, d), lambda b: (b, 0)),
            pl.BlockSpec((None, LANES, d), lambda b: (layer, 3 * FOX_WIDTH // LANES, 0)),
            pl.BlockSpec((1, LANES), lambda b: (0, 0)),
        ],
        out_specs=pl.BlockSpec((seq, LANES), lambda b: (b, 0)),
        out_shape=jax.ShapeDtypeStruct((m, LANES), f32),
        compiler_params=_params("parallel"),
        name="fox_gate",
    )(hn, w_stack, jnp.pad(b_f, (0, LANES - nh)).reshape(1, LANES))


ATTN_BLOCK = 512


def _fox_attn_kernel(q_ref, k_ref, v_ref, csum_ref, o_ref, qa_ref, ka_ref):
    seq = q_ref.shape[0]
    t = ATTN_BLOCK
    hd = FOX_HEAD_DIM
    head = pl.program_id(1)
    lane = lax.broadcasted_iota(jnp.int32, (seq, LANES), 1)
    c = jnp.sum(jnp.where(lane == head, csum_ref[...], 0.0), axis=1, keepdims=True)
    hi = c.astype(bf16).astype(f32)
    r1 = c - hi
    mid = r1.astype(bf16).astype(f32)
    lo = r1 - mid
    ext_q = jnp.where(lane == 0, hi, jnp.where(lane == 1, mid, jnp.where(lane == 2, lo,
                                                                         jnp.where(lane < 6, 1.0, 0.0))))
    ext_k = jnp.where(lane < 3, 1.0, jnp.where(lane == 3, -hi, jnp.where(lane == 4, -mid,
                                                                         jnp.where(lane == 5, -lo, 0.0))))
    qa_ref[:, :hd] = q_ref[...]
    qa_ref[:, hd:] = ext_q.astype(bf16)
    ka_ref[:, :hd] = k_ref[...]
    ka_ref[:, hd:] = ext_k.astype(bf16)

    causal = lax.broadcasted_iota(jnp.int32, (t, t), 0) >= lax.broadcasted_iota(jnp.int32, (t, t), 1)
    for qi in range(seq // t):
        q = qa_ref[qi * t:(qi + 1) * t, :]
        for j in range(qi + 1):
            s = _dot_nt(q, ka_ref[j * t:(j + 1) * t, :])
            if j == qi:
                s = jnp.where(causal, s, -jnp.inf)
            v = v_ref[j * t:(j + 1) * t, :]
            m_blk = jnp.max(s, axis=1, keepdims=True)
            if j == 0:
                m_run = m_blk
                p = jnp.exp2(s - m_run)
                l_run = jnp.sum(p, axis=1, keepdims=True)
                acc = _dot(p.astype(bf16), v)
            else:
                m_new = jnp.maximum(m_run, m_blk)
                alpha = jnp.exp2(m_run - m_new)
                p = jnp.exp2(s - m_new)
                l_run = alpha * l_run + jnp.sum(p, axis=1, keepdims=True)
                acc = alpha * acc + _dot(p.astype(bf16), v)
                m_run = m_new
        o_ref[qi * t:(qi + 1) * t, :] = (acc * (1.0 / l_run)).astype(o_ref.dtype)


def _fox_attention(qkv, csum, batch, seq):
    m = qkv.shape[0]
    nh = FOX_N_HEADS
    hd = FOX_HEAD_DIM
    return pl.pallas_call(
        _fox_attn_kernel,
        grid=(batch, nh),
        in_specs=[
            pl.BlockSpec((seq, hd), lambda b, h: (b, h)),
            pl.BlockSpec((seq, hd), lambda b, h: (b, nh + h)),
            pl.BlockSpec((seq, hd), lambda b, h: (b, 2 * nh + h)),
            pl.BlockSpec((seq, LANES), lambda b, h: (b, 0)),
        ],
        out_specs=pl.BlockSpec((seq, hd), lambda b, h: (b, h)),
        out_shape=jax.ShapeDtypeStruct((m, FOX_WIDTH), bf16),
        scratch_shapes=[pltpu.VMEM((seq, 2 * hd), bf16), pltpu.VMEM((seq, 2 * hd), bf16)],
        compiler_params=_params("parallel", "parallel"),
        name="fox_attn",
    )(qkv, qkv, qkv, csum)


def kernel(x, p, norm_mix_pre, norm_mix_post, norm_ffn_pre, norm_ffn_post, ssd_w_in, ssd_conv_w, ssd_conv_b, ssd_dt_bias, ssd_a_log, ssd_d, ssd_norm_w, ssd_w_out, fox_w_in, fox_b_f, fox_w_out, ffn_w_gate, ffn_w_up, ffn_w_down, ple_w_proj, ple_norm, ple_w_gate):
    batch, seq, d = x.shape
    depth = p.shape[0]
    m = batch * seq
    h = x.reshape(m, d)
    p_stack = p.reshape(depth, m, PLE_DIM)
    ssd_scale = jnp.ones((1, SSD_D_INNER), f32)
    fox_scale = jnp.concatenate(
        [jnp.full((1, FOX_WIDTH), FOX_HEAD_DIM ** -0.5 * LOG2E, f32), jnp.ones((1, 2 * FOX_WIDTH), f32)], axis=1)

    ssd_w_in = jnp.swapaxes(ssd_w_in, 1, 2)
    fox_w_in = jnp.swapaxes(fox_w_in, 1, 2)

    hn = _rmsnorm_bf16(h, norm_mix_pre, 0)
    for i in range(depth):
        j = i // 2
        if i % 2 == 0:
            z = _in_proj(hn, ssd_w_in, j, SSD_D_INNER, ssd_scale, "ssd_in_proj_z")
            xbc = _in_proj_conv(hn, ssd_w_in, ssd_conv_w, ssd_conv_b, j, SSD_D_INNER, SSD_CONV_DIM, seq)
            dt, acum, acum_t = _ssd_dt(hn, ssd_w_in, j, ssd_dt_bias[j], ssd_a_log[j])
            y = _ssd_scan(xbc, dt, acum, acum_t, ssd_d[j], batch, seq)
            h, hn = _ssd_out(y, z, ssd_norm_w, ssd_w_out, j, h, norm_mix_post, norm_ffn_pre, i)
        else:
            qkv = _in_proj(hn, fox_w_in, j, 3 * FOX_WIDTH, fox_scale, "fox_in_proj")
            csum = _fox_gate(hn, fox_w_in, j, fox_b_f[j], batch, seq)
            mixed = _fox_attention(qkv, csum, batch, seq)
            h, hn = _matmul_residual(mixed, fox_w_out, j, h, norm_mix_post, norm_ffn_pre, i, "norm", "fox_out_proj",
                                     tm=512)
        act = _ffn_up(hn, ffn_w_gate, ffn_w_up, i)
        h, hb = _matmul_residual(act, ffn_w_down, i, h, norm_ffn_post, norm_ffn_post, i, "cast", "ffn_down")
        last = i == depth - 1
        outs = _ple(hb, h, p_stack, ple_w_gate, ple_w_proj, ple_norm, norm_mix_pre, i, 0 if last else i + 1,
                    "none" if last else "norm")
        h = outs[0]
        if not last:
            hn = outs[1]
    return h.reshape(batch, seq, d)
```

```python
import functools
import math

import jax
import jax.numpy as jnp
from jax import lax
from jax.experimental import pallas as pl
from jax.experimental.pallas import tpu as pltpu

f32 = jnp.float32
bf16 = jnp.bfloat16

D_MODEL = 2048
EPS = 1e-6

SSD_D_INNER = 4096
SSD_HEAD_DIM = 64
SSD_N_HEADS = 64
SSD_N_GROUPS = 8
SSD_HEADS_PER_GROUP = 8
SSD_D_STATE = 128
SSD_CONV_K = 4
SSD_CHUNK = 128
SSD_GROUP_WIDTH = SSD_HEADS_PER_GROUP * SSD_HEAD_DIM
SSD_BC_WIDTH = SSD_N_GROUPS * SSD_D_STATE
SSD_CONV_DIM = SSD_D_INNER + 2 * SSD_BC_WIDTH
SSD_ZX_DIM = SSD_D_INNER + SSD_CONV_DIM

FOX_N_HEADS = 16
FOX_HEAD_DIM = 128
FOX_WIDTH = 2048

FFN_HIDDEN = 5632
PLE_DIM = 256

LANES = 128
VMEM_LIMIT_BYTES = 48 * 1024 * 1024
VMEM_LIMIT_RESIDENT_BYTES = 58 * 1024 * 1024


def _params(*semantics, vmem=VMEM_LIMIT_BYTES):
    return pltpu.CompilerParams(dimension_semantics=semantics, vmem_limit_bytes=vmem)


def _dot(a, b):
    return jnp.dot(a, b, preferred_element_type=f32)


def _dot_nt(a, b):
    return lax.dot_general(a, b, (((1,), (1,)), ((), ())), preferred_element_type=f32)


def _dot_tn(a, b):
    return lax.dot_general(a, b, (((0,), (0,)), ((), ())), preferred_element_type=f32)


def _rms(x, gain):
    return x * lax.rsqrt(jnp.mean(x * x, axis=-1, keepdims=True) + EPS) * gain


def _sigmoid(x):
    return 1.0 / (1.0 + jnp.exp(-x))


def _silu(x):
    half = 0.5 * x
    return half + half * jnp.tanh(half)


def _softplus(x):
    return jnp.maximum(x, 0.0) + jnp.log1p(jnp.exp(-jnp.abs(x)))


def _split3(x):
    hi = x.astype(bf16)
    r1 = x - hi.astype(f32)
    mid = r1.astype(bf16)
    lo = (r1 - mid.astype(f32)).astype(bf16)
    return hi, mid, lo


def _tri(n, lower):
    r = lax.broadcasted_iota(jnp.int32, (n, n), 0)
    c = lax.broadcasted_iota(jnp.int32, (n, n), 1)
    return jnp.where((r >= c) if lower else (r <= c), 1.0, 0.0).astype(bf16)


def _rmsnorm_kernel(x_ref, g_ref, o_ref):
    o_ref[...] = _rms(x_ref[...], g_ref[...]).astype(o_ref.dtype)


def _rmsnorm_bf16(x, gains, layer, tm=512):
    m, d = x.shape
    return pl.pallas_call(
        _rmsnorm_kernel,
        grid=(m // tm,),
        in_specs=[pl.BlockSpec((tm, d), lambda i: (i, 0)), pl.BlockSpec((None, 1, d), lambda i: (layer, 0, 0))],
        out_specs=pl.BlockSpec((tm, d), lambda i: (i, 0)),
        out_shape=jax.ShapeDtypeStruct((m, d), bf16),
        compiler_params=_params("parallel"),
        name="rmsnorm",
    )(x, gains.reshape(gains.shape[0], 1, d))


def _in_proj_kernel(x_ref, w_ref, scale_ref, o_ref, wb_ref):
    @pl.when(pl.program_id(1) == 0)
    def _():
        wb_ref[...] = w_ref[...].astype(bf16)

    o_ref[...] = (_dot_nt(x_ref[...], wb_ref[...]) * scale_ref[...]).astype(o_ref.dtype)


def _in_proj(x, w_stack, layer, n_cols, col_scale, name, tm=2048, tn=1024):
    m, k = x.shape
    return pl.pallas_call(
        _in_proj_kernel,
        grid=(n_cols // tn, m // tm),
        in_specs=[
            pl.BlockSpec((tm, k), lambda j, i: (i, 0)),
            pl.BlockSpec((None, tn, k), lambda j, i: (layer, j, 0)),
            pl.BlockSpec((1, tn), lambda j, i: (0, j)),
        ],
        out_specs=pl.BlockSpec((tm, tn), lambda j, i: (i, j)),
        out_shape=jax.ShapeDtypeStruct((m, n_cols), bf16),
        scratch_shapes=[pltpu.VMEM((tn, k), bf16)],
        compiler_params=_params("parallel", "arbitrary", vmem=VMEM_LIMIT_RESIDENT_BYTES),
        name=name,
    )(x, w_stack, col_scale)


CONV_HALO = 8
CONV_ROWS = 256


def _in_proj_conv_kernel(x_ref, w_ref, cw_ref, cb_ref, o_ref, wb_ref, pad_ref, *, tiles_per_seq):
    i = pl.program_id(1)
    tm = x_ref.shape[0]

    @pl.when(i == 0)
    def _():
        wb_ref[...] = w_ref[...].astype(bf16)

    @pl.when(i % tiles_per_seq == 0)
    def _():
        pad_ref[0:CONV_HALO, :] = jnp.zeros((CONV_HALO, pad_ref.shape[1]), f32)

    pad_ref[CONV_HALO:CONV_HALO + tm, :] = _dot_nt(x_ref[...], wb_ref[...])
    for r in range(tm // CONV_ROWS):
        base = CONV_HALO + r * CONV_ROWS
        acc = cb_ref[...] + cw_ref[SSD_CONV_K - 1:SSD_CONV_K, :] * pad_ref[base:base + CONV_ROWS, :]
        for j in range(SSD_CONV_K - 1):
            shift = SSD_CONV_K - 1 - j
            acc = acc + cw_ref[j:j + 1, :] * pad_ref[base - shift:base - shift + CONV_ROWS, :]
        o_ref[r * CONV_ROWS:(r + 1) * CONV_ROWS, :] = _silu(acc).astype(o_ref.dtype)
    pad_ref[0:CONV_HALO, :] = pad_ref[tm:tm + CONV_HALO, :]


def _in_proj_conv(x, w_stack, conv_w, conv_b, layer, col0, n_cols, seq, tm=1024, tn=1024):
    m, k = x.shape
    ck = conv_w.shape[1]
    return pl.pallas_call(
        functools.partial(_in_proj_conv_kernel, tiles_per_seq=seq // tm),
        grid=(n_cols // tn, m // tm),
        in_specs=[
            pl.BlockSpec((tm, k), lambda j, i: (i, 0)),
            pl.BlockSpec((None, tn, k), lambda j, i: (layer, col0 // tn + j, 0)),
            pl.BlockSpec((None, ck, tn), lambda j, i: (layer, 0, j)),
            pl.BlockSpec((None, 1, tn), lambda j, i: (layer, 0, j)),
        ],
        out_specs=pl.BlockSpec((tm, tn), lambda j, i: (i, j)),
        out_shape=jax.ShapeDtypeStruct((m, n_cols), bf16),
        scratch_shapes=[pltpu.VMEM((tn, k), bf16), pltpu.VMEM((CONV_HALO + tm, tn), f32)],
        compiler_params=_params("parallel", "arbitrary"),
        name="ssd_in_proj_conv",
    )(x, w_stack, conv_w, conv_b.reshape(conv_b.shape[0], 1, n_cols))


def _ffn_up_kernel(x_ref, wg_ref, wu_ref, o_ref, wgb_ref, wub_ref):
    @pl.when(pl.program_id(1) == 0)
    def _():
        wgb_ref[...] = wg_ref[...].astype(bf16)
        wub_ref[...] = wu_ref[...].astype(bf16)

    x = x_ref[...]
    g = _dot(x, wgb_ref[...])
    u = _dot(x, wub_ref[...])
    o_ref[...] = (_silu(g) * u).astype(o_ref.dtype)


def _ffn_up(x, wg_stack, wu_stack, layer, tm=1024, tn=512):
    m, k = x.shape
    n = wg_stack.shape[2]
    w_spec = pl.BlockSpec((None, k, tn), lambda j, i: (layer, 0, j))
    return pl.pallas_call(
        _ffn_up_kernel,
        grid=(n // tn, m // tm),
        in_specs=[pl.BlockSpec((tm, k), lambda j, i: (i, 0)), w_spec, w_spec],
        out_specs=pl.BlockSpec((tm, tn), lambda j, i: (i, j)),
        out_shape=jax.ShapeDtypeStruct((m, n), bf16),
        scratch_shapes=[pltpu.VMEM((k, tn), bf16), pltpu.VMEM((k, tn), bf16)],
        compiler_params=_params("parallel", "arbitrary", vmem=VMEM_LIMIT_RESIDENT_BYTES),
        name="ffn_up",
    )(x, wg_stack, wu_stack)


WEIGHT_CHUNK_ROWS = 256


def _load_weight_bf16(w_hbm, wb_ref, stage_ref, sem_ref):
    rows = WEIGHT_CHUNK_ROWS
    n_chunks = wb_ref.shape[0] // rows

    def chunk_copy(c):
        return pltpu.make_async_copy(w_hbm.at[pl.ds(c * rows, rows)], stage_ref.at[c % 2], sem_ref.at[c % 2])

    chunk_copy(0).start()
    for c in range(n_chunks):
        if c + 1 < n_chunks:
            chunk_copy(c + 1).start()
        chunk_copy(c).wait()
        wb_ref[c * rows:(c + 1) * rows, :] = stage_ref[c % 2].astype(bf16)


def _weight_scratch(k, n):
    return [pltpu.VMEM((k, n), bf16), pltpu.VMEM((2, WEIGHT_CHUNK_ROWS, n), f32), pltpu.SemaphoreType.DMA((2,))]


def _next_activation(hnew, gnext_ref, next_mode):
    if next_mode == "norm":
        return _rms(hnew, gnext_ref[...]).astype(bf16)
    return hnew.astype(bf16)


def _mm_res_kernel(x_ref, w_hbm, h_ref, gpost_ref, gnext_ref, hout_ref, hn_ref, wb_ref, stage_ref, sem_ref, *,
                   layer, next_mode):
    @pl.when(pl.program_id(0) == 0)
    def _():
        _load_weight_bf16(w_hbm.at[layer], wb_ref, stage_ref, sem_ref)

    mix = _dot(x_ref[...], wb_ref[...])
    hnew = h_ref[...] + _rms(mix, gpost_ref[...])
    hout_ref[...] = hnew
    hn_ref[...] = _next_activation(hnew, gnext_ref, next_mode)


def _ssd_out_kernel(y_ref, z_ref, normw_ref, w_hbm, h_ref, gpost_ref, gnext_ref, hout_ref, hn_ref,
                    wb_ref, stage_ref, sem_ref, *, layer):
    @pl.when(pl.program_id(0) == 0)
    def _():
        _load_weight_bf16(w_hbm.at[layer], wb_ref, stage_ref, sem_ref)

    gw = SSD_GROUP_WIDTH
    parts = []
    for g in range(SSD_N_GROUPS):
        cols = slice(g * gw, (g + 1) * gw)
        z = z_ref[:, cols].astype(f32)
        y = y_ref[:, cols].astype(f32) * _silu(z)
        parts.append(_rms(y, normw_ref[:, cols]).astype(bf16))
    mix = _dot(jnp.concatenate(parts, axis=1), wb_ref[...])
    hnew = h_ref[...] + _rms(mix, gpost_ref[...])
    hout_ref[...] = hnew
    hn_ref[...] = _next_activation(hnew, gnext_ref, "norm")


def _gain_spec(d, layer):
    return pl.BlockSpec((None, 1, d), lambda i: (layer, 0, 0))


def _gains3(g):
    return g.reshape(g.shape[0], 1, g.shape[1])


def _ssd_out(y, z, norm_w, w_stack, w_layer, h, gpost, gnext, layer, tm=256):
    m, k = y.shape
    d = h.shape[1]
    row = lambda i: (i, 0)
    return pl.pallas_call(
        functools.partial(_ssd_out_kernel, layer=w_layer),
        grid=(m // tm,),
        in_specs=[
            pl.BlockSpec((tm, k), row),
            pl.BlockSpec((tm, k), row),
            _gain_spec(k, w_layer),
            pl.BlockSpec(memory_space=pl.ANY),
            pl.BlockSpec((tm, d), row),
            _gain_spec(d, layer),
            _gain_spec(d, layer),
        ],
        out_specs=[pl.BlockSpec((tm, d), row), pl.BlockSpec((tm, d), row)],
        out_shape=[jax.ShapeDtypeStruct((m, d), f32), jax.ShapeDtypeStruct((m, d), bf16)],
        scratch_shapes=_weight_scratch(k, d),
        compiler_params=_params("arbitrary", vmem=VMEM_LIMIT_RESIDENT_BYTES),
        name="ssd_out_proj",
    )(y, z, _gains3(norm_w), w_stack, h, _gains3(gpost), _gains3(gnext))


def _matmul_residual(x, w_stack, w_layer, h, gpost, gnext, layer, next_mode, name, tm=256):
    m, k = x.shape
    d = h.shape[1]
    row = lambda i: (i, 0)
    return pl.pallas_call(
        functools.partial(_mm_res_kernel, layer=w_layer, next_mode=next_mode),
        grid=(m // tm,),
        in_specs=[
            pl.BlockSpec((tm, k), row),
            pl.BlockSpec(memory_space=pl.ANY),
            pl.BlockSpec((tm, d), row),
            _gain_spec(d, layer),
            _gain_spec(d, layer),
        ],
        out_specs=[pl.BlockSpec((tm, d), row), pl.BlockSpec((tm, d), row)],
        out_shape=[jax.ShapeDtypeStruct((m, d), f32), jax.ShapeDtypeStruct((m, d), bf16)],
        scratch_shapes=_weight_scratch(k, d),
        compiler_params=_params("arbitrary", vmem=VMEM_LIMIT_RESIDENT_BYTES),
        name=name,
    )(x, w_stack, h, _gains3(gpost), _gains3(gnext))


def _ple_kernel(hb_ref, h_ref, p_ref, wgate_hbm, wproj_ref, gple_ref, gnext_ref, *refs, layer, next_mode):
    n_out = 1 if next_mode == "none" else 2
    out_refs = refs[:n_out]
    wb_ref, stage_ref, sem_ref = refs[n_out:]

    @pl.when(pl.program_id(0) == 0)
    def _():
        _load_weight_bf16(wgate_hbm.at[layer], wb_ref, stage_ref, sem_ref)

    gate = _dot(hb_ref[...], wb_ref[...])
    pe = _rms(_dot(p_ref[...].astype(bf16), wproj_ref[...].astype(bf16)), gple_ref[...])
    hnew = h_ref[...] + _sigmoid(gate) * pe
    out_refs[0][...] = hnew
    if next_mode != "none":
        out_refs[1][...] = _next_activation(hnew, gnext_ref, next_mode)


def _ple(hb, h, p_stack, wgate_stack, wproj_stack, gple, gnext, layer, next_layer, next_mode, tm=512):
    m, d = h.shape
    pd = p_stack.shape[2]
    row = lambda i: (i, 0)
    out_specs = [pl.BlockSpec((tm, d), row)]
    out_shape = [jax.ShapeDtypeStruct((m, d), f32)]
    if next_mode != "none":
        out_specs.append(pl.BlockSpec((tm, d), row))
        out_shape.append(jax.ShapeDtypeStruct((m, d), bf16))
    return pl.pallas_call(
        functools.partial(_ple_kernel, layer=layer, next_mode=next_mode),
        grid=(m // tm,),
        in_specs=[
            pl.BlockSpec((tm, d), row),
            pl.BlockSpec((tm, d), row),
            pl.BlockSpec((None, tm, pd), lambda i: (layer, i, 0)),
            pl.BlockSpec(memory_space=pl.ANY),
            pl.BlockSpec((None, pd, d), lambda i: (layer, 0, 0), pipeline_mode=pl.Buffered(1)),
            _gain_spec(d, layer),
            _gain_spec(d, next_layer),
        ],
        out_specs=out_specs,
        out_shape=out_shape,
        scratch_shapes=_weight_scratch(d, d),
        compiler_params=_params("arbitrary", vmem=VMEM_LIMIT_RESIDENT_BYTES),
        name="ple",
    )(hb, h, p_stack, wgate_stack, wproj_stack, _gains3(gple), _gains3(gnext))


LOG2E = math.log2(math.e)


def _narrow_weight(w_ref, n_valid):
    row = lax.broadcasted_iota(jnp.int32, w_ref.shape, 0)
    return jnp.where(row < n_valid, w_ref[...], 0.0).astype(bf16)


def _ssd_dt_kernel(hn_ref, w_ref, bias_ref, alog_ref, dt_ref, acum_ref, acum_t_ref):
    hn = hn_ref[...]
    tm = hn.shape[0]
    q = SSD_CHUNK
    dt = _softplus(_dot_nt(hn, _narrow_weight(w_ref, SSD_N_HEADS)) + bias_ref[...])
    dt_ref[...] = dt
    adt = dt * (-jnp.exp(alog_ref[...]))
    tril = _tri(q, lower=True)
    for c in range(tm // q):
        hi, mid, lo = _split3(adt[c * q:(c + 1) * q, :])
        acum = (_dot(tril, hi) + _dot(tril, mid) + _dot(tril, lo)) * LOG2E
        acum_ref[c * q:(c + 1) * q, :] = acum
        acum_t_ref[:, c * q:(c + 1) * q] = acum.T[:SSD_N_HEADS, :]


def _ssd_dt(hn, w_stack, layer, dt_bias, a_log, tm=1024):
    m, d = hn.shape
    nh = SSD_N_HEADS
    pad_row = lambda v: jnp.pad(v, (0, LANES - nh)).reshape(1, LANES)
    fixed = lambda i: (0, 0)
    return pl.pallas_call(
        _ssd_dt_kernel,
        grid=(m // tm,),
        in_specs=[
            pl.BlockSpec((tm, d), lambda i: (i, 0)),
            pl.BlockSpec((None, LANES, d), lambda i: (layer, SSD_ZX_DIM // LANES, 0)),
            pl.BlockSpec((1, LANES), fixed),
            pl.BlockSpec((1, LANES), fixed),
        ],
        out_specs=[
            pl.BlockSpec((tm, LANES), lambda i: (i, 0)),
            pl.BlockSpec((tm, LANES), lambda i: (i, 0)),
            pl.BlockSpec((nh, tm), lambda i: (0, i)),
        ],
        out_shape=[
            jax.ShapeDtypeStruct((m, LANES), f32),
            jax.ShapeDtypeStruct((m, LANES), f32),
            jax.ShapeDtypeStruct((nh, m), f32),
        ],
        compiler_params=_params("parallel"),
        name="ssd_dt",
    )(hn, w_stack, pad_row(dt_bias), pad_row(a_log))


def _ssd_scan_kernel(xs_ref, b_ref, c_ref, dt_ref, acum_ref, acum_t_ref, dskip_ref, o_ref, state_ref):
    q = SSD_CHUNK
    gw = SSD_GROUP_WIDTH
    hd = SSD_HEAD_DIM

    @pl.when(pl.program_id(1) == 0)
    def _():
        state_ref[...] = jnp.zeros_like(state_ref)

    rows = lax.broadcasted_iota(jnp.int32, (q, q), 0)
    cols = lax.broadcasted_iota(jnp.int32, (q, q), 1)
    tril = rows >= cols
    first_head = cols < hd

    for g in range(SSD_N_GROUPS):
        bg = b_ref[:, g * SSD_D_STATE:(g + 1) * SSD_D_STATE]
        cg = c_ref[:, g * SSD_D_STATE:(g + 1) * SSD_D_STATE]
        cb = _dot_nt(cg, bg)
        xg = xs_ref[:, g * gw:(g + 1) * gw].astype(f32)

        acol = []
        dt_pairs = []
        ac_pairs = []
        for i in range(SSD_HEADS_PER_GROUP):
            h = g * SSD_HEADS_PER_GROUP + i
            acol.append(jnp.broadcast_to(acum_ref[:, h:h + 1], (q, q)))
        for pr in range(SSD_HEADS_PER_GROUP // 2):
            h0 = g * SSD_HEADS_PER_GROUP + 2 * pr
            d0 = jnp.broadcast_to(dt_ref[:, h0:h0 + 1], (q, q))
            d1 = jnp.broadcast_to(dt_ref[:, h0 + 1:h0 + 2], (q, q))
            dt_pairs.append(jnp.where(first_head, d0, d1))
            ac_pairs.append(jnp.where(first_head, acol[2 * pr], acol[2 * pr + 1]))
        dt_g = jnp.concatenate(dt_pairs, axis=1)
        ac_g = jnp.concatenate(ac_pairs, axis=1)
        a_last = ac_g[q - 1:q, :]

        xdt = xg * dt_g
        st_new = _dot_tn(bg, (xdt * jnp.exp2(a_last - ac_g)).astype(bf16))
        prev = state_ref[g]
        y = _dot(cg, prev.astype(bf16)) * jnp.exp2(ac_g)
        state_ref[g] = prev * jnp.exp2(a_last) + st_new

        y_pairs = []
        for pr in range(SSD_HEADS_PER_GROUP // 2):
            xp = xdt[:, pr * q:(pr + 1) * q]
            rhs = jnp.concatenate(
                [jnp.where(first_head, xp, 0.0).astype(bf16), jnp.where(first_head, 0.0, xp).astype(bf16)], axis=0)
            lhs = []
            for k in range(2):
                i = 2 * pr + k
                h = g * SSD_HEADS_PER_GROUP + i
                seg = acol[i] - acum_t_ref[h:h + 1, :]
                decay = jnp.exp2(jnp.where(tril, seg, -jnp.inf))
                lhs.append((cb * decay).astype(bf16))
            y_pairs.append(_dot(jnp.concatenate(lhs, axis=1), rhs))
        y = y + jnp.concatenate(y_pairs, axis=1) + xg * dskip_ref[:, g * gw:(g + 1) * gw]
        o_ref[:, g * gw:(g + 1) * gw] = y.astype(o_ref.dtype)


def _ssd_scan(xbc, dt, acum, acum_t, d_skip, batch, seq):
    m = xbc.shape[0]
    q = SSD_CHUNK
    nc = seq // q
    di = SSD_D_INNER
    bw = SSD_BC_WIDTH
    chunk = lambda b, c: (b * nc + c, 0)
    fixed = lambda b, c: (0, 0)
    d_cols = jnp.repeat(d_skip, SSD_HEAD_DIM).reshape(1, di)
    return pl.pallas_call(
        _ssd_scan_kernel,
        grid=(batch, nc),
        in_specs=[
            pl.BlockSpec((q, di), chunk),
            pl.BlockSpec((q, bw), lambda b, c: (b * nc + c, di // bw)),
            pl.BlockSpec((q, bw), lambda b, c: (b * nc + c, di // bw + 1)),
            pl.BlockSpec((q, LANES), chunk),
            pl.BlockSpec((q, LANES), chunk),
            pl.BlockSpec((SSD_N_HEADS, q), lambda b, c: (0, b * nc + c)),
            pl.BlockSpec((1, di), fixed),
        ],
        out_specs=pl.BlockSpec((q, di), chunk),
        out_shape=jax.ShapeDtypeStruct((m, di), bf16),
        scratch_shapes=[pltpu.VMEM((SSD_N_GROUPS, SSD_D_STATE, SSD_GROUP_WIDTH), f32)],
        compiler_params=_params("parallel", "arbitrary"),
        name="ssd_scan",
    )(xbc, xbc, xbc, dt, acum, acum_t, d_cols)


def _fox_gate_kernel(hn_ref, w_ref, b_row_ref, csum_ref):
    hn = hn_ref[...]
    seq = hn.shape[0]
    q = LANES
    logf = -_softplus(-(_dot_nt(hn, _narrow_weight(w_ref, FOX_N_HEADS)) + b_row_ref[...]))
    tril = _tri(q, lower=True)
    carry = jnp.zeros((1, LANES), f32)
    for c in range(seq // q):
        hi, mid, lo = _split3(logf[c * q:(c + 1) * q, :])
        cs = _dot(tril, hi) + _dot(tril, mid) + _dot(tril, lo) + carry
        csum_ref[c * q:(c + 1) * q, :] = cs * LOG2E
        carry = cs[q - 1:q, :]


def _fox_gate(hn, w_stack, layer, b_f, batch, seq):
    m, d = hn.shape
    nh = FOX_N_HEADS
    return pl.pallas_call(
        _fox_gate_kernel,
        grid=(batch,),
        in_specs=[
            pl.BlockSpec((seq, d), lambda b: (b, 0)),
            pl.BlockSpec((None, LANES, d), lambda b: (layer, 3 * FOX_WIDTH // LANES, 0)),
            pl.BlockSpec((1, LANES), lambda b: (0, 0)),
        ],
        out_specs=pl.BlockSpec((seq, LANES), lambda b: (b, 0)),
        out_shape=jax.ShapeDtypeStruct((m, LANES), f32),
        compiler_params=_params("parallel"),
        name="fox_gate",
    )(hn, w_stack, jnp.pad(b_f, (0, LANES - nh)).reshape(1, LANES))


ATTN_BLOCK = 512


def _fox_attn_kernel(q_ref, k_ref, v_ref, csum_ref, o_ref, qa_ref, ka_ref, va_ref):
    seq = q_ref.shape[0]
    t = ATTN_BLOCK
    hd = FOX_HEAD_DIM
    head = pl.program_id(1)
    lane = lax.broadcasted_iota(jnp.int32, (seq, LANES), 1)
    c = jnp.sum(jnp.where(lane == head, csum_ref[...], 0.0), axis=1, keepdims=True)
    hi = c.astype(bf16).astype(f32)
    r1 = c - hi
    mid = r1.astype(bf16).astype(f32)
    lo = r1 - mid
    ext_q = jnp.where(lane == 0, hi, jnp.where(lane == 1, mid, jnp.where(lane == 2, lo,
                                                                         jnp.where(lane < 6, 1.0, 0.0))))
    ext_k = jnp.where(lane < 3, 1.0, jnp.where(lane == 3, -hi, jnp.where(lane == 4, -mid,
                                                                         jnp.where(lane == 5, -lo, 0.0))))
    qa_ref[:, :hd] = q_ref[...]
    qa_ref[:, hd:] = ext_q.astype(bf16)
    ka_ref[:, :hd] = k_ref[...]
    ka_ref[:, hd:] = ext_k.astype(bf16)
    va_ref[:, :hd] = v_ref[...]
    va_ref[:, hd:] = jnp.ones((seq, hd), bf16)

    causal = lax.broadcasted_iota(jnp.int32, (t, t), 0) >= lax.broadcasted_iota(jnp.int32, (t, t), 1)
    for qi in range(seq // t):
        q = qa_ref[qi * t:(qi + 1) * t, :]
        for j in range(qi + 1):
            s = _dot_nt(q, ka_ref[j * t:(j + 1) * t, :])
            if j == qi:
                s = jnp.where(causal, s, -jnp.inf)
            v = va_ref[j * t:(j + 1) * t, :]
            m_blk = jnp.max(s, axis=1, keepdims=True)
            if j == 0:
                m_run = m_blk
                acc = _dot(jnp.exp2(s - m_run).astype(bf16), v)
            else:
                m_new = jnp.maximum(m_run, m_blk)
                acc = jnp.exp2(m_run - m_new) * acc + _dot(jnp.exp2(s - m_new).astype(bf16), v)
                m_run = m_new
        o_ref[qi * t:(qi + 1) * t, :] = (acc[:, :hd] * (1.0 / acc[:, hd:hd + 1])).astype(o_ref.dtype)


def _fox_attention(qkv, csum, batch, seq):
    m = qkv.shape[0]
    nh = FOX_N_HEADS
    hd = FOX_HEAD_DIM
    return pl.pallas_call(
        _fox_attn_kernel,
        grid=(batch, nh),
        in_specs=[
            pl.BlockSpec((seq, hd), lambda b, h: (b, h)),
            pl.BlockSpec((seq, hd), lambda b, h: (b, nh + h)),
            pl.BlockSpec((seq, hd), lambda b, h: (b, 2 * nh + h)),
            pl.BlockSpec((seq, LANES), lambda b, h: (b, 0)),
        ],
        out_specs=pl.BlockSpec((seq, hd), lambda b, h: (b, h)),
        out_shape=jax.ShapeDtypeStruct((m, FOX_WIDTH), bf16),
        scratch_shapes=[pltpu.VMEM((seq, 2 * hd), bf16)] * 3,
        compiler_params=_params("parallel", "parallel"),
        name="fox_attn",
    )(qkv, qkv, qkv, csum)


def kernel(x, p, norm_mix_pre, norm_mix_post, norm_ffn_pre, norm_ffn_post, ssd_w_in, ssd_conv_w, ssd_conv_b, ssd_dt_bias, ssd_a_log, ssd_d, ssd_norm_w, ssd_w_out, fox_w_in, fox_b_f, fox_w_out, ffn_w_gate, ffn_w_up, ffn_w_down, ple_w_proj, ple_norm, ple_w_gate):
    batch, seq, d = x.shape
    depth = p.shape[0]
    m = batch * seq
    h = x.reshape(m, d)
    p_stack = p.reshape(depth, m, PLE_DIM)
    ssd_scale = jnp.ones((1, SSD_D_INNER), f32)
    fox_scale = jnp.concatenate(
        [jnp.full((1, FOX_WIDTH), FOX_HEAD_DIM ** -0.5 * LOG2E, f32), jnp.ones((1, 2 * FOX_WIDTH), f32)], axis=1)

    ssd_w_in = jnp.swapaxes(ssd_w_in, 1, 2)
    fox_w_in = jnp.swapaxes(fox_w_in, 1, 2)

    hn = _rmsnorm_bf16(h, norm_mix_pre, 0)
    for i in range(depth):
        j = i // 2
        if i % 2 == 0:
            z = _in_proj(hn, ssd_w_in, j, SSD_D_INNER, ssd_scale, "ssd_in_proj_z")
            xbc = _in_proj_conv(hn, ssd_w_in, ssd_conv_w, ssd_conv_b, j, SSD_D_INNER, SSD_CONV_DIM, seq)
            dt, acum, acum_t = _ssd_dt(hn, ssd_w_in, j, ssd_dt_bias[j], ssd_a_log[j])
            y = _ssd_scan(xbc, dt, acum, acum_t, ssd_d[j], batch, seq)
            h, hn = _ssd_out(y, z, ssd_norm_w, ssd_w_out, j, h, norm_mix_post, norm_ffn_pre, i)
        else:
            qkv = _in_proj(hn, fox_w_in, j, 3 * FOX_WIDTH, fox_scale, "fox_in_proj")
            csum = _fox_gate(hn, fox_w_in, j, fox_b_f[j], batch, seq)
            mixed = _fox_attention(qkv, csum, batch, seq)
            h, hn = _matmul_residual(mixed, fox_w_out, j, h, norm_mix_post, norm_ffn_pre, i, "norm", "fox_out_proj",
                                     tm=512)
        act = _ffn_up(hn, ffn_w_gate, ffn_w_up, i)
        h, hb = _matmul_residual(act, ffn_w_down, i, h, norm_ffn_post, norm_ffn_post, i, "cast", "ffn_down")
        last = i == depth - 1
        outs = _ple(hb, h, p_stack, ple_w_gate, ple_w_proj, ple_norm, norm_mix_pre, i, 0 if last else i + 1,
                    "none" if last else "norm")
        h = outs[0]
        if not last:
            hn = outs[1]
    return h.reshape(batch, seq, d)
```

```python
import functools
import math

import jax
import jax.numpy as jnp
from jax import lax
from jax.experimental import pallas as pl
from jax.experimental.pallas import tpu as pltpu

f32 = jnp.float32
bf16 = jnp.bfloat16

D_MODEL = 2048
EPS = 1e-6

SSD_D_INNER = 4096
SSD_HEAD_DIM = 64
SSD_N_HEADS = 64
SSD_N_GROUPS = 8
SSD_HEADS_PER_GROUP = 8
SSD_D_STATE = 128
SSD_CONV_K = 4
SSD_CHUNK = 128
SSD_GROUP_WIDTH = SSD_HEADS_PER_GROUP * SSD_HEAD_DIM
SSD_BC_WIDTH = SSD_N_GROUPS * SSD_D_STATE
SSD_CONV_DIM = SSD_D_INNER + 2 * SSD_BC_WIDTH
SSD_ZX_DIM = SSD_D_INNER + SSD_CONV_DIM

FOX_N_HEADS = 16
FOX_HEAD_DIM = 128
FOX_WIDTH = 2048

FFN_HIDDEN = 5632
PLE_DIM = 256

LANES = 128
VMEM_LIMIT_BYTES = 48 * 1024 * 1024
VMEM_LIMIT_RESIDENT_BYTES = 58 * 1024 * 1024


def _params(*semantics, vmem=VMEM_LIMIT_BYTES):
    return pltpu.CompilerParams(dimension_semantics=semantics, vmem_limit_bytes=vmem)


def _dot(a, b):
    return jnp.dot(a, b, preferred_element_type=f32)


def _dot_nt(a, b):
    return lax.dot_general(a, b, (((1,), (1,)), ((), ())), preferred_element_type=f32)


def _dot_tn(a, b):
    return lax.dot_general(a, b, (((0,), (0,)), ((), ())), preferred_element_type=f32)


def _rms(x, gain):
    return x * lax.rsqrt(jnp.mean(x * x, axis=-1, keepdims=True) + EPS) * gain


def _sigmoid(x):
    return 0.5 + 0.5 * jnp.tanh(0.5 * x)


def _silu(x):
    half = 0.5 * x
    return half + half * jnp.tanh(half)


def _softplus(x):
    return jnp.maximum(x, 0.0) + jnp.log1p(jnp.exp(-jnp.abs(x)))


def _split3(x):
    hi = x.astype(bf16)
    r1 = x - hi.astype(f32)
    mid = r1.astype(bf16)
    lo = (r1 - mid.astype(f32)).astype(bf16)
    return hi, mid, lo


def _tri(n, lower):
    r = lax.broadcasted_iota(jnp.int32, (n, n), 0)
    c = lax.broadcasted_iota(jnp.int32, (n, n), 1)
    return jnp.where((r >= c) if lower else (r <= c), 1.0, 0.0).astype(bf16)


def _in_proj_kernel(x_ref, w_ref, scale_ref, o_ref, wb_ref):
    @pl.when(pl.program_id(1) == 0)
    def _():
        wb_ref[...] = w_ref[...].astype(bf16)

    o_ref[...] = (_dot_nt(x_ref[...], wb_ref[...]) * scale_ref[...]).astype(o_ref.dtype)


def _in_proj(x, w_stack, layer, n_cols, col_scale, name, tm=2048, tn=1024):
    m, k = x.shape
    return pl.pallas_call(
        _in_proj_kernel,
        grid=(n_cols // tn, m // tm),
        in_specs=[
            pl.BlockSpec((tm, k), lambda j, i: (i, 0)),
            pl.BlockSpec((None, tn, k), lambda j, i: (layer, j, 0)),
            pl.BlockSpec((1, tn), lambda j, i: (0, j)),
        ],
        out_specs=pl.BlockSpec((tm, tn), lambda j, i: (i, j)),
        out_shape=jax.ShapeDtypeStruct((m, n_cols), bf16),
        scratch_shapes=[pltpu.VMEM((tn, k), bf16)],
        compiler_params=_params("parallel", "arbitrary", vmem=VMEM_LIMIT_RESIDENT_BYTES),
        name=name,
    )(x, w_stack, col_scale)


CONV_HALO = 8
CONV_ROWS = 256


def _in_proj_conv_kernel(x_ref, w_ref, cw_ref, cb_ref, o_ref, wb_ref, pad_ref, *, tiles_per_seq):
    i = pl.program_id(1)
    tm = x_ref.shape[0]

    @pl.when(i == 0)
    def _():
        wb_ref[...] = w_ref[...].astype(bf16)

    @pl.when(i % tiles_per_seq == 0)
    def _():
        pad_ref[0:CONV_HALO, :] = jnp.zeros((CONV_HALO, pad_ref.shape[1]), f32)

    pad_ref[CONV_HALO:CONV_HALO + tm, :] = _dot_nt(x_ref[...], wb_ref[...])
    for r in range(tm // CONV_ROWS):
        base = CONV_HALO + r * CONV_ROWS
        acc = cb_ref[...] + cw_ref[SSD_CONV_K - 1:SSD_CONV_K, :] * pad_ref[base:base + CONV_ROWS, :]
        for j in range(SSD_CONV_K - 1):
            shift = SSD_CONV_K - 1 - j
            acc = acc + cw_ref[j:j + 1, :] * pad_ref[base - shift:base - shift + CONV_ROWS, :]
        o_ref[r * CONV_ROWS:(r + 1) * CONV_ROWS, :] = _silu(acc).astype(o_ref.dtype)
    pad_ref[0:CONV_HALO, :] = pad_ref[tm:tm + CONV_HALO, :]


def _in_proj_conv(x, w_stack, conv_w, conv_b, layer, col0, n_cols, seq, tm=1024, tn=1024):
    m, k = x.shape
    ck = conv_w.shape[1]
    return pl.pallas_call(
        functools.partial(_in_proj_conv_kernel, tiles_per_seq=seq // tm),
        grid=(n_cols // tn, m // tm),
        in_specs=[
            pl.BlockSpec((tm, k), lambda j, i: (i, 0)),
            pl.BlockSpec((None, tn, k), lambda j, i: (layer, col0 // tn + j, 0)),
            pl.BlockSpec((None, ck, tn), lambda j, i: (layer, 0, j)),
            pl.BlockSpec((None, 1, tn), lambda j, i: (layer, 0, j)),
        ],
        out_specs=pl.BlockSpec((tm, tn), lambda j, i: (i, j)),
        out_shape=jax.ShapeDtypeStruct((m, n_cols), bf16),
        scratch_shapes=[pltpu.VMEM((tn, k), bf16), pltpu.VMEM((CONV_HALO + tm, tn), f32)],
        compiler_params=_params("parallel", "arbitrary"),
        name="ssd_in_proj_conv",
    )(x, w_stack, conv_w, conv_b.reshape(conv_b.shape[0], 1, n_cols))


def _ffn_up_kernel(x_ref, wg_ref, wu_ref, o_ref, wgb_ref, wub_ref):
    @pl.when(pl.program_id(1) == 0)
    def _():
        wgb_ref[...] = wg_ref[...].astype(bf16)
        wub_ref[...] = wu_ref[...].astype(bf16)

    x = x_ref[...]
    g = _dot(x, wgb_ref[...])
    u = _dot(x, wub_ref[...])
    o_ref[...] = (_silu(g) * u).astype(o_ref.dtype)


def _ffn_up(x, wg_stack, wu_stack, layer, tm=1024, tn=512):
    m, k = x.shape
    n = wg_stack.shape[2]
    w_spec = pl.BlockSpec((None, k, tn), lambda j, i: (layer, 0, j))
    return pl.pallas_call(
        _ffn_up_kernel,
        grid=(n // tn, m // tm),
        in_specs=[pl.BlockSpec((tm, k), lambda j, i: (i, 0)), w_spec, w_spec],
        out_specs=pl.BlockSpec((tm, tn), lambda j, i: (i, j)),
        out_shape=jax.ShapeDtypeStruct((m, n), bf16),
        scratch_shapes=[pltpu.VMEM((k, tn), bf16), pltpu.VMEM((k, tn), bf16)],
        compiler_params=_params("parallel", "arbitrary", vmem=VMEM_LIMIT_RESIDENT_BYTES),
        name="ffn_up",
    )(x, wg_stack, wu_stack)


WEIGHT_CHUNK_ROWS = 256


def _load_weight_bf16(w_hbm, wb_ref, stage_ref, sem_ref):
    rows = WEIGHT_CHUNK_ROWS
    n_chunks = wb_ref.shape[0] // rows

    def chunk_copy(c):
        return pltpu.make_async_copy(w_hbm.at[pl.ds(c * rows, rows)], stage_ref.at[c % 2], sem_ref.at[c % 2])

    chunk_copy(0).start()
    for c in range(n_chunks):
        if c + 1 < n_chunks:
            chunk_copy(c + 1).start()
        chunk_copy(c).wait()
        wb_ref[c * rows:(c + 1) * rows, :] = stage_ref[c % 2].astype(bf16)


def _weight_scratch(k, n):
    return [pltpu.VMEM((k, n), bf16), pltpu.VMEM((2, WEIGHT_CHUNK_ROWS, n), f32), pltpu.SemaphoreType.DMA((2,))]


def _next_activation(hnew, gnext_ref, next_mode):
    if next_mode == "norm":
        return _rms(hnew, gnext_ref[...]).astype(bf16)
    return hnew.astype(bf16)


def _mm_res_kernel(x_ref, w_hbm, h_ref, gpost_ref, gnext_ref, hout_ref, hn_ref, wb_ref, stage_ref, sem_ref, *,
                   layer, next_mode):
    @pl.when(pl.program_id(0) == 0)
    def _():
        _load_weight_bf16(w_hbm.at[layer], wb_ref, stage_ref, sem_ref)

    mix = _dot(x_ref[...], wb_ref[...])
    hnew = h_ref[...] + _rms(mix, gpost_ref[...])
    hout_ref[...] = hnew
    hn_ref[...] = _next_activation(hnew, gnext_ref, next_mode)


def _ssd_out_kernel(y_ref, z_ref, normw_ref, w_hbm, h_ref, gpost_ref, gnext_ref, hout_ref, hn_ref,
                    wb_ref, stage_ref, sem_ref, *, layer):
    @pl.when(pl.program_id(0) == 0)
    def _():
        _load_weight_bf16(w_hbm.at[layer], wb_ref, stage_ref, sem_ref)

    gw = SSD_GROUP_WIDTH
    parts = []
    for g in range(SSD_N_GROUPS):
        cols = slice(g * gw, (g + 1) * gw)
        z = z_ref[:, cols].astype(f32)
        y = y_ref[:, cols].astype(f32) * _silu(z)
        parts.append(_rms(y, normw_ref[:, cols]).astype(bf16))
    mix = _dot(jnp.concatenate(parts, axis=1), wb_ref[...])
    hnew = h_ref[...] + _rms(mix, gpost_ref[...])
    hout_ref[...] = hnew
    hn_ref[...] = _next_activation(hnew, gnext_ref, "norm")


def _gain_spec(d, layer):
    return pl.BlockSpec((None, 1, d), lambda i: (layer, 0, 0))


def _gains3(g):
    return g.reshape(g.shape[0], 1, g.shape[1])


def _ssd_out(y, z, norm_w, w_stack, w_layer, h, gpost, gnext, layer, tm=256):
    m, k = y.shape
    d = h.shape[1]
    row = lambda i: (i, 0)
    return pl.pallas_call(
        functools.partial(_ssd_out_kernel, layer=w_layer),
        grid=(m // tm,),
        in_specs=[
            pl.BlockSpec((tm, k), row),
            pl.BlockSpec((tm, k), row),
            _gain_spec(k, w_layer),
            pl.BlockSpec(memory_space=pl.ANY),
            pl.BlockSpec((tm, d), row),
            _gain_spec(d, layer),
            _gain_spec(d, layer),
        ],
        out_specs=[pl.BlockSpec((tm, d), row), pl.BlockSpec((tm, d), row)],
        out_shape=[jax.ShapeDtypeStruct((m, d), f32), jax.ShapeDtypeStruct((m, d), bf16)],
        scratch_shapes=_weight_scratch(k, d),
        compiler_params=_params("arbitrary", vmem=VMEM_LIMIT_RESIDENT_BYTES),
        name="ssd_out_proj",
    )(y, z, _gains3(norm_w), w_stack, h, _gains3(gpost), _gains3(gnext))


def _matmul_residual(x, w_stack, w_layer, h, gpost, gnext, layer, next_mode, name, tm=256):
    m, k = x.shape
    d = h.shape[1]
    row = lambda i: (i, 0)
    return pl.pallas_call(
        functools.partial(_mm_res_kernel, layer=w_layer, next_mode=next_mode),
        grid=(m // tm,),
        in_specs=[
            pl.BlockSpec((tm, k), row),
            pl.BlockSpec(memory_space=pl.ANY),
            pl.BlockSpec((tm, d), row),
            _gain_spec(d, layer),
            _gain_spec(d, layer),
        ],
        out_specs=[pl.BlockSpec((tm, d), row), pl.BlockSpec((tm, d), row)],
        out_shape=[jax.ShapeDtypeStruct((m, d), f32), jax.ShapeDtypeStruct((m, d), bf16)],
        scratch_shapes=_weight_scratch(k, d),
        compiler_params=_params("arbitrary", vmem=VMEM_LIMIT_RESIDENT_BYTES),
        name=name,
    )(x, w_stack, h, _gains3(gpost), _gains3(gnext))


def _ple_kernel(hb_ref, h_ref, p_ref, wgate_hbm, wproj_ref, gple_ref, gnext_ref, *refs, layer, next_mode):
    n_out = 1 if next_mode == "none" else 2
    out_refs = refs[:n_out]
    wb_ref, stage_ref, sem_ref = refs[n_out:]

    @pl.when(pl.program_id(0) == 0)
    def _():
        _load_weight_bf16(wgate_hbm.at[layer], wb_ref, stage_ref, sem_ref)

    gate = _dot(hb_ref[...], wb_ref[...])
    pe = _rms(_dot(p_ref[...].astype(bf16), wproj_ref[...].astype(bf16)), gple_ref[...])
    hnew = h_ref[...] + _sigmoid(gate) * pe
    out_refs[0][...] = hnew
    if next_mode != "none":
        out_refs[1][...] = _next_activation(hnew, gnext_ref, next_mode)


def _ple(hb, h, p_stack, wgate_stack, wproj_stack, gple, gnext, layer, next_layer, next_mode, tm=512):
    m, d = h.shape
    pd = p_stack.shape[2]
    row = lambda i: (i, 0)
    out_specs = [pl.BlockSpec((tm, d), row)]
    out_shape = [jax.ShapeDtypeStruct((m, d), f32)]
    if next_mode != "none":
        out_specs.append(pl.BlockSpec((tm, d), row))
        out_shape.append(jax.ShapeDtypeStruct((m, d), bf16))
    return pl.pallas_call(
        functools.partial(_ple_kernel, layer=layer, next_mode=next_mode),
        grid=(m // tm,),
        in_specs=[
            pl.BlockSpec((tm, d), row),
            pl.BlockSpec((tm, d), row),
            pl.BlockSpec((None, tm, pd), lambda i: (layer, i, 0)),
            pl.BlockSpec(memory_space=pl.ANY),
            pl.BlockSpec((None, pd, d), lambda i: (layer, 0, 0), pipeline_mode=pl.Buffered(1)),
            _gain_spec(d, layer),
            _gain_spec(d, next_layer),
        ],
        out_specs=out_specs,
        out_shape=out_shape,
        scratch_shapes=_weight_scratch(d, d),
        compiler_params=_params("arbitrary", vmem=VMEM_LIMIT_RESIDENT_BYTES),
        name="ple",
    )(hb, h, p_stack, wgate_stack, wproj_stack, _gains3(gple), _gains3(gnext))


LOG2E = math.log2(math.e)


def _narrow_weight(w_ref, n_valid):
    row = lax.broadcasted_iota(jnp.int32, w_ref.shape, 0)
    return jnp.where(row < n_valid, w_ref[...], 0.0).astype(bf16)


def _ssd_dt_kernel(*refs, pre_norm):
    if pre_norm:
        x_ref, gain_ref, w_ref, bias_ref, alog_ref, hn_ref, dt_ref, acum_ref, acum_t_ref = refs
        hn = _rms(x_ref[...], gain_ref[...]).astype(bf16)
        hn_ref[...] = hn
    else:
        hn_ref, w_ref, bias_ref, alog_ref, dt_ref, acum_ref, acum_t_ref = refs
        hn = hn_ref[...]
    tm = hn.shape[0]
    q = SSD_CHUNK
    dt = _softplus(_dot_nt(hn, _narrow_weight(w_ref, SSD_N_HEADS)) + bias_ref[...])
    dt_ref[...] = dt
    adt = dt * (-jnp.exp(alog_ref[...]))
    tril = _tri(q, lower=True)
    for c in range(tm // q):
        hi, mid, lo = _split3(adt[c * q:(c + 1) * q, :])
        acum = (_dot(tril, hi) + _dot(tril, mid) + _dot(tril, lo)) * LOG2E
        acum_ref[c * q:(c + 1) * q, :] = acum
        acum_t_ref[:, c * q:(c + 1) * q] = acum.T[:SSD_N_HEADS, :]


def _ssd_dt(x, w_stack, layer, dt_bias, a_log, pre_norm_gain=None, tm=1024):
    m, d = x.shape
    nh = SSD_N_HEADS
    pad_row = lambda v: jnp.pad(v, (0, LANES - nh)).reshape(1, LANES)
    fixed = lambda i: (0, 0)
    row = lambda i: (i, 0)
    pre_norm = pre_norm_gain is not None
    args = [x] + ([pre_norm_gain] if pre_norm else []) + [w_stack, pad_row(dt_bias), pad_row(a_log)]
    in_specs = [pl.BlockSpec((tm, d), row)] + ([pl.BlockSpec((1, d), fixed)] if pre_norm else []) + [
        pl.BlockSpec((None, LANES, d), lambda i: (layer, SSD_ZX_DIM // LANES, 0)),
        pl.BlockSpec((1, LANES), fixed),
        pl.BlockSpec((1, LANES), fixed),
    ]
    out_specs = ([pl.BlockSpec((tm, d), row)] if pre_norm else []) + [
        pl.BlockSpec((tm, LANES), row),
        pl.BlockSpec((tm, LANES), row),
        pl.BlockSpec((nh, tm), lambda i: (0, i)),
    ]
    out_shape = ([jax.ShapeDtypeStruct((m, d), bf16)] if pre_norm else []) + [
        jax.ShapeDtypeStruct((m, LANES), f32),
        jax.ShapeDtypeStruct((m, LANES), f32),
        jax.ShapeDtypeStruct((nh, m), f32),
    ]
    return pl.pallas_call(
        functools.partial(_ssd_dt_kernel, pre_norm=pre_norm),
        grid=(m // tm,),
        in_specs=in_specs,
        out_specs=out_specs,
        out_shape=out_shape,
        compiler_params=_params("parallel"),
        name="ssd_dt",
    )(*args)


def _ssd_scan_kernel(xs_ref, b_ref, c_ref, dt_ref, acum_ref, acum_t_ref, dskip_ref, o_ref, state_ref, expand_ref):
    q = SSD_CHUNK
    gw = SSD_GROUP_WIDTH
    hd = SSD_HEAD_DIM
    nh = SSD_N_HEADS

    @pl.when(pl.program_id(1) == 0)
    def _():
        state_ref[...] = jnp.zeros_like(state_ref)
        k = lax.broadcasted_iota(jnp.int32, expand_ref.shape, 0)
        col = lax.broadcasted_iota(jnp.int32, expand_ref.shape, 1)
        expand_ref[...] = jnp.where((k < 3 * nh) & (k % nh == col // hd), 1.0, 0.0).astype(bf16)

    rows = lax.broadcasted_iota(jnp.int32, (q, q), 0)
    cols = lax.broadcasted_iota(jnp.int32, (q, q), 1)
    tril = rows >= cols
    first_head = cols < hd
    dt_hi, dt_mid, dt_lo = _split3(dt_ref[...])
    dt_parts = jnp.concatenate([dt_hi[:, :nh], dt_mid[:, :nh], dt_lo[:, :nh], jnp.zeros((q, nh), bf16)], axis=1)

    for g in range(SSD_N_GROUPS):
        bg = b_ref[:, g * SSD_D_STATE:(g + 1) * SSD_D_STATE]
        cg = c_ref[:, g * SSD_D_STATE:(g + 1) * SSD_D_STATE]
        cb = _dot_nt(cg, bg)
        xg = xs_ref[:, g * gw:(g + 1) * gw].astype(f32)

        acol = []
        ac_pairs = []
        for i in range(SSD_HEADS_PER_GROUP):
            h = g * SSD_HEADS_PER_GROUP + i
            acol.append(jnp.broadcast_to(acum_ref[:, h:h + 1], (q, q)))
        for pr in range(SSD_HEADS_PER_GROUP // 2):
            ac_pairs.append(jnp.where(first_head, acol[2 * pr], acol[2 * pr + 1]))
        dt_g = _dot(dt_parts, expand_ref[:, g * gw:(g + 1) * gw])
        ac_g = jnp.concatenate(ac_pairs, axis=1)
        a_last = ac_g[q - 1:q, :]

        xdt = xg * dt_g
        st_new = _dot_tn(bg, (xdt * jnp.exp2(a_last - ac_g)).astype(bf16))
        prev = state_ref[g]
        y = _dot(cg, prev.astype(bf16)) * jnp.exp2(ac_g)
        state_ref[g] = prev * jnp.exp2(a_last) + st_new

        y_pairs = []
        for pr in range(SSD_HEADS_PER_GROUP // 2):
            xp = xdt[:, pr * q:(pr + 1) * q]
            rhs = jnp.concatenate(
                [jnp.where(first_head, xp, 0.0).astype(bf16), jnp.where(first_head, 0.0, xp).astype(bf16)], axis=0)
            lhs = []
            for k in range(2):
                i = 2 * pr + k
                h = g * SSD_HEADS_PER_GROUP + i
                seg = acol[i] - acum_t_ref[h:h + 1, :]
                decay = jnp.exp2(jnp.where(tril, seg, -jnp.inf))
                lhs.append((cb * decay).astype(bf16))
            y_pairs.append(_dot(jnp.concatenate(lhs, axis=1), rhs))
        y = y + jnp.concatenate(y_pairs, axis=1) + xg * dskip_ref[:, g * gw:(g + 1) * gw]
        o_ref[:, g * gw:(g + 1) * gw] = y.astype(o_ref.dtype)


def _ssd_scan(xbc, dt, acum, acum_t, d_skip, batch, seq):
    m = xbc.shape[0]
    q = SSD_CHUNK
    nc = seq // q
    di = SSD_D_INNER
    bw = SSD_BC_WIDTH
    chunk = lambda b, c: (b * nc + c, 0)
    fixed = lambda b, c: (0, 0)
    d_cols = jnp.repeat(d_skip, SSD_HEAD_DIM).reshape(1, di)
    return pl.pallas_call(
        _ssd_scan_kernel,
        grid=(batch, nc),
        in_specs=[
            pl.BlockSpec((q, di), chunk),
            pl.BlockSpec((q, bw), lambda b, c: (b * nc + c, di // bw)),
            pl.BlockSpec((q, bw), lambda b, c: (b * nc + c, di // bw + 1)),
            pl.BlockSpec((q, LANES), chunk),
            pl.BlockSpec((q, LANES), chunk),
            pl.BlockSpec((SSD_N_HEADS, q), lambda b, c: (0, b * nc + c)),
            pl.BlockSpec((1, di), fixed),
        ],
        out_specs=pl.BlockSpec((q, di), chunk),
        out_shape=jax.ShapeDtypeStruct((m, di), bf16),
        scratch_shapes=[pltpu.VMEM((SSD_N_GROUPS, SSD_D_STATE, SSD_GROUP_WIDTH), f32),
                        pltpu.VMEM((4 * SSD_N_HEADS, di), bf16)],
        compiler_params=_params("parallel", "arbitrary"),
        name="ssd_scan",
    )(xbc, xbc, xbc, dt, acum, acum_t, d_cols)


def _fox_gate_kernel(hn_ref, w_ref, b_row_ref, csum_ref):
    hn = hn_ref[...]
    seq = hn.shape[0]
    q = LANES
    logf = -_softplus(-(_dot_nt(hn, _narrow_weight(w_ref, FOX_N_HEADS)) + b_row_ref[...]))
    tril = _tri(q, lower=True)
    carry = jnp.zeros((1, LANES), f32)
    for c in range(seq // q):
        hi, mid, lo = _split3(logf[c * q:(c + 1) * q, :])
        cs = _dot(tril, hi) + _dot(tril, mid) + _dot(tril, lo) + carry
        csum_ref[c * q:(c + 1) * q, :] = cs * LOG2E
        carry = cs[q - 1:q, :]


def _fox_gate(hn, w_stack, layer, b_f, batch, seq):
    m, d = hn.shape
    nh = FOX_N_HEADS
    return pl.pallas_call(
        _fox_gate_kernel,
        grid=(batch,),
        in_specs=[
            pl.BlockSpec((seq, d), lambda b: (b, 0)),
            pl.BlockSpec((None, LANES, d), lambda b: (layer, 3 * FOX_WIDTH // LANES, 0)),
            pl.BlockSpec((1, LANES), lambda b: (0, 0)),
        ],
        out_specs=pl.BlockSpec((seq, LANES), lambda b: (b, 0)),
        out_shape=jax.ShapeDtypeStruct((m, LANES), f32),
        compiler_params=_params("parallel"),
        name="fox_gate",
    )(hn, w_stack, jnp.pad(b_f, (0, LANES - nh)).reshape(1, LANES))


ATTN_BLOCK = 512


def _fox_attn_kernel(q_ref, k_ref, v_ref, csum_ref, o_ref, qa_ref, ka_ref, va_ref):
    seq = q_ref.shape[0]
    t = ATTN_BLOCK
    hd = FOX_HEAD_DIM
    head = pl.program_id(1)
    lane = lax.broadcasted_iota(jnp.int32, (seq, LANES), 1)
    c = jnp.sum(jnp.where(lane == head, csum_ref[...], 0.0), axis=1, keepdims=True)
    hi = c.astype(bf16).astype(f32)
    r1 = c - hi
    mid = r1.astype(bf16).astype(f32)
    lo = r1 - mid
    ext_q = jnp.where(lane == 0, hi, jnp.where(lane == 1, mid, jnp.where(lane == 2, lo,
                                                                         jnp.where(lane < 6, 1.0, 0.0))))
    ext_k = jnp.where(lane < 3, 1.0, jnp.where(lane == 3, -hi, jnp.where(lane == 4, -mid,
                                                                         jnp.where(lane == 5, -lo, 0.0))))
    qa_ref[:, :hd] = q_ref[...]
    qa_ref[:, hd:] = ext_q.astype(bf16)
    ka_ref[:, :hd] = k_ref[...]
    ka_ref[:, hd:] = ext_k.astype(bf16)
    va_ref[:, :hd] = v_ref[...]
    va_ref[:, hd:] = jnp.ones((seq, hd), bf16)

    causal = lax.broadcasted_iota(jnp.int32, (t, t), 0) >= lax.broadcasted_iota(jnp.int32, (t, t), 1)
    for qi in range(seq // t):
        q = qa_ref[qi * t:(qi + 1) * t, :]
        for j in range(qi + 1):
            s = _dot_nt(q, ka_ref[j * t:(j + 1) * t, :])
            if j == qi:
                s = jnp.where(causal, s, -jnp.inf)
            v = va_ref[j * t:(j + 1) * t, :]
            m_blk = jnp.max(s, axis=1, keepdims=True)
            if j == 0:
                m_run = m_blk
                acc = _dot(jnp.exp2(s - m_run).astype(bf16), v)
            else:
                m_new = jnp.maximum(m_run, m_blk)
                acc = jnp.exp2(m_run - m_new) * acc + _dot(jnp.exp2(s - m_new).astype(bf16), v)
                m_run = m_new
        o_ref[qi * t:(qi + 1) * t, :] = (acc[:, :hd] * (1.0 / acc[:, hd:hd + 1])).astype(o_ref.dtype)


def _fox_attention(qkv, csum, batch, seq):
    m = qkv.shape[0]
    nh = FOX_N_HEADS
    hd = FOX_HEAD_DIM
    return pl.pallas_call(
        _fox_attn_kernel,
        grid=(batch, nh),
        in_specs=[
            pl.BlockSpec((seq, hd), lambda b, h: (b, h)),
            pl.BlockSpec((seq, hd), lambda b, h: (b, nh + h)),
            pl.BlockSpec((seq, hd), lambda b, h: (b, 2 * nh + h)),
            pl.BlockSpec((seq, LANES), lambda b, h: (b, 0)),
        ],
        out_specs=pl.BlockSpec((seq, hd), lambda b, h: (b, h)),
        out_shape=jax.ShapeDtypeStruct((m, FOX_WIDTH), bf16),
        scratch_shapes=[pltpu.VMEM((seq, 2 * hd), bf16)] * 3,
        compiler_params=_params("parallel", "parallel"),
        name="fox_attn",
    )(qkv, qkv, qkv, csum)


def kernel(x, p, norm_mix_pre, norm_mix_post, norm_ffn_pre, norm_ffn_post, ssd_w_in, ssd_conv_w, ssd_conv_b, ssd_dt_bias, ssd_a_log, ssd_d, ssd_norm_w, ssd_w_out, fox_w_in, fox_b_f, fox_w_out, ffn_w_gate, ffn_w_up, ffn_w_down, ple_w_proj, ple_norm, ple_w_gate):
    batch, seq, d = x.shape
    depth = p.shape[0]
    m = batch * seq
    h = x.reshape(m, d)
    p_stack = p.reshape(depth, m, PLE_DIM)
    ssd_scale = jnp.ones((1, SSD_D_INNER), f32)
    fox_scale = jnp.concatenate(
        [jnp.full((1, FOX_WIDTH), FOX_HEAD_DIM ** -0.5 * LOG2E, f32), jnp.ones((1, 2 * FOX_WIDTH), f32)], axis=1)

    ssd_w_in = jnp.swapaxes(ssd_w_in, 1, 2)
    fox_w_in = jnp.swapaxes(fox_w_in, 1, 2)

    hn = None
    for i in range(depth):
        j = i // 2
        if i % 2 == 0:
            if i == 0:
                hn, dt, acum, acum_t = _ssd_dt(h, ssd_w_in, j, ssd_dt_bias[j], ssd_a_log[j],
                                               pre_norm_gain=norm_mix_pre[0:1])
            else:
                dt, acum, acum_t = _ssd_dt(hn, ssd_w_in, j, ssd_dt_bias[j], ssd_a_log[j])
            z = _in_proj(hn, ssd_w_in, j, SSD_D_INNER, ssd_scale, "ssd_in_proj_z")
            xbc = _in_proj_conv(hn, ssd_w_in, ssd_conv_w, ssd_conv_b, j, SSD_D_INNER, SSD_CONV_DIM, seq)
            y = _ssd_scan(xbc, dt, acum, acum_t, ssd_d[j], batch, seq)
            h, hn = _ssd_out(y, z, ssd_norm_w, ssd_w_out, j, h, norm_mix_post, norm_ffn_pre, i)
        else:
            qkv = _in_proj(hn, fox_w_in, j, 3 * FOX_WIDTH, fox_scale, "fox_in_proj")
            csum = _fox_gate(hn, fox_w_in, j, fox_b_f[j], batch, seq)
            mixed = _fox_attention(qkv, csum, batch, seq)
            h, hn = _matmul_residual(mixed, fox_w_out, j, h, norm_mix_post, norm_ffn_pre, i, "norm", "fox_out_proj",
                                     tm=512)
        act = _ffn_up(hn, ffn_w_gate, ffn_w_up, i)
        h, hb = _matmul_residual(act, ffn_w_down, i, h, norm_ffn_post, norm_ffn_post, i, "cast", "ffn_down")
        last = i == depth - 1
        outs = _ple(hb, h, p_stack, ple_w_gate, ple_w_proj, ple_norm, norm_mix_pre, i, 0 if last else i + 1,
                    "none" if last else "norm")
        h = outs[0]
        if not last:
            hn = outs[1]
    return h.reshape(batch, seq, d)
```

```python
import functools
import math

import jax
import jax.numpy as jnp
from jax import lax
from jax.experimental import pallas as pl
from jax.experimental.pallas import tpu as pltpu

f32 = jnp.float32
bf16 = jnp.bfloat16

D_MODEL = 2048
EPS = 1e-6

SSD_D_INNER = 4096
SSD_HEAD_DIM = 64
SSD_N_HEADS = 64
SSD_N_GROUPS = 8
SSD_HEADS_PER_GROUP = 8
SSD_D_STATE = 128
SSD_CONV_K = 4
SSD_CHUNK = 128
SSD_GROUP_WIDTH = SSD_HEADS_PER_GROUP * SSD_HEAD_DIM
SSD_BC_WIDTH = SSD_N_GROUPS * SSD_D_STATE
SSD_CONV_DIM = SSD_D_INNER + 2 * SSD_BC_WIDTH
SSD_ZX_DIM = SSD_D_INNER + SSD_CONV_DIM

FOX_N_HEADS = 16
FOX_HEAD_DIM = 128
FOX_WIDTH = 2048

FFN_HIDDEN = 5632
PLE_DIM = 256

LANES = 128
VMEM_LIMIT_BYTES = 48 * 1024 * 1024
VMEM_LIMIT_RESIDENT_BYTES = 58 * 1024 * 1024


def _params(*semantics, vmem=VMEM_LIMIT_BYTES):
    return pltpu.CompilerParams(dimension_semantics=semantics, vmem_limit_bytes=vmem)


def _dot(a, b):
    return jnp.dot(a, b, preferred_element_type=f32)


def _dot_nt(a, b):
    return lax.dot_general(a, b, (((1,), (1,)), ((), ())), preferred_element_type=f32)


def _dot_tn(a, b):
    return lax.dot_general(a, b, (((0,), (0,)), ((), ())), preferred_element_type=f32)


def _rms(x, gain):
    return x * lax.rsqrt(jnp.mean(x * x, axis=-1, keepdims=True) + EPS) * gain


def _sigmoid(x):
    return 0.5 + 0.5 * jnp.tanh(0.5 * x)


def _silu(x):
    half = 0.5 * x
    return half + half * jnp.tanh(half)


def _softplus(x):
    return jnp.maximum(x, 0.0) + jnp.log1p(jnp.exp(-jnp.abs(x)))


def _split3(x):
    hi = x.astype(bf16)
    r1 = x - hi.astype(f32)
    mid = r1.astype(bf16)
    lo = (r1 - mid.astype(f32)).astype(bf16)
    return hi, mid, lo


def _tri(n, lower):
    r = lax.broadcasted_iota(jnp.int32, (n, n), 0)
    c = lax.broadcasted_iota(jnp.int32, (n, n), 1)
    return jnp.where((r >= c) if lower else (r <= c), 1.0, 0.0).astype(bf16)


def _in_proj_kernel(x_ref, w_ref, scale_ref, o_ref, wb_ref):
    @pl.when(pl.program_id(1) == 0)
    def _():
        wb_ref[...] = w_ref[...].astype(bf16)

    o_ref[...] = (_dot_nt(x_ref[...], wb_ref[...]) * scale_ref[...]).astype(o_ref.dtype)


def _in_proj(x, w_stack, layer, n_cols, col_scale, name, tm=2048, tn=1024):
    m, k = x.shape
    return pl.pallas_call(
        _in_proj_kernel,
        grid=(n_cols // tn, m // tm),
        in_specs=[
            pl.BlockSpec((tm, k), lambda j, i: (i, 0)),
            pl.BlockSpec((None, tn, k), lambda j, i: (layer, j, 0)),
            pl.BlockSpec((1, tn), lambda j, i: (0, j)),
        ],
        out_specs=pl.BlockSpec((tm, tn), lambda j, i: (i, j)),
        out_shape=jax.ShapeDtypeStruct((m, n_cols), bf16),
        scratch_shapes=[pltpu.VMEM((tn, k), bf16)],
        compiler_params=_params("parallel", "arbitrary", vmem=VMEM_LIMIT_RESIDENT_BYTES),
        name=name,
    )(x, w_stack, col_scale)


CONV_HALO = 8
CONV_ROWS = 256


def _in_proj_conv_kernel(x_ref, w_ref, cw_ref, cb_ref, o_ref, wb_ref, pad_ref, *, tiles_per_seq):
    i = pl.program_id(1)
    tm = x_ref.shape[0]

    @pl.when(i == 0)
    def _():
        wb_ref[...] = w_ref[...].astype(bf16)

    @pl.when(i % tiles_per_seq == 0)
    def _():
        pad_ref[0:CONV_HALO, :] = jnp.zeros((CONV_HALO, pad_ref.shape[1]), f32)

    pad_ref[CONV_HALO:CONV_HALO + tm, :] = _dot_nt(x_ref[...], wb_ref[...])
    for r in range(tm // CONV_ROWS):
        base = CONV_HALO + r * CONV_ROWS
        window = pad_ref[base - CONV_HALO:base + CONV_ROWS, :]
        acc = cb_ref[...] + cw_ref[SSD_CONV_K - 1:SSD_CONV_K, :] * window[CONV_HALO:, :]
        for j in range(SSD_CONV_K - 1):
            shift = SSD_CONV_K - 1 - j
            acc = acc + cw_ref[j:j + 1, :] * pltpu.roll(window, shift, axis=0)[CONV_HALO:, :]
        o_ref[r * CONV_ROWS:(r + 1) * CONV_ROWS, :] = _silu(acc).astype(o_ref.dtype)
    pad_ref[0:CONV_HALO, :] = pad_ref[tm:tm + CONV_HALO, :]


def _in_proj_conv(x, w_stack, conv_w, conv_b, layer, col0, n_cols, seq, tm=1024, tn=1024):
    m, k = x.shape
    ck = conv_w.shape[1]
    return pl.pallas_call(
        functools.partial(_in_proj_conv_kernel, tiles_per_seq=seq // tm),
        grid=(n_cols // tn, m // tm),
        in_specs=[
            pl.BlockSpec((tm, k), lambda j, i: (i, 0)),
            pl.BlockSpec((None, tn, k), lambda j, i: (layer, col0 // tn + j, 0)),
            pl.BlockSpec((None, ck, tn), lambda j, i: (layer, 0, j)),
            pl.BlockSpec((None, 1, tn), lambda j, i: (layer, 0, j)),
        ],
        out_specs=pl.BlockSpec((tm, tn), lambda j, i: (i, j)),
        out_shape=jax.ShapeDtypeStruct((m, n_cols), bf16),
        scratch_shapes=[pltpu.VMEM((tn, k), bf16), pltpu.VMEM((CONV_HALO + tm, tn), f32)],
        compiler_params=_params("parallel", "arbitrary"),
        name="ssd_in_proj_conv",
    )(x, w_stack, conv_w, conv_b.reshape(conv_b.shape[0], 1, n_cols))


def _ffn_up_kernel(x_ref, wg_ref, wu_ref, o_ref, wgb_ref, wub_ref):
    @pl.when(pl.program_id(1) == 0)
    def _():
        wgb_ref[...] = wg_ref[...].astype(bf16)
        wub_ref[...] = wu_ref[...].astype(bf16)

    x = x_ref[...]
    g = _dot(x, wgb_ref[...])
    u = _dot(x, wub_ref[...])
    o_ref[...] = (_silu(g) * u).astype(o_ref.dtype)


def _ffn_up(x, wg_stack, wu_stack, layer, tm=1024, tn=512):
    m, k = x.shape
    n = wg_stack.shape[2]
    w_spec = pl.BlockSpec((None, k, tn), lambda j, i: (layer, 0, j))
    return pl.pallas_call(
        _ffn_up_kernel,
        grid=(n // tn, m // tm),
        in_specs=[pl.BlockSpec((tm, k), lambda j, i: (i, 0)), w_spec, w_spec],
        out_specs=pl.BlockSpec((tm, tn), lambda j, i: (i, j)),
        out_shape=jax.ShapeDtypeStruct((m, n), bf16),
        scratch_shapes=[pltpu.VMEM((k, tn), bf16), pltpu.VMEM((k, tn), bf16)],
        compiler_params=_params("parallel", "arbitrary", vmem=VMEM_LIMIT_RESIDENT_BYTES),
        name="ffn_up",
    )(x, wg_stack, wu_stack)


WEIGHT_CHUNK_ROWS = 256
WEIGHT_SLOTS = 4


def _load_weight_bf16(w_hbm, wb_ref, stage_ref, sem_ref):
    rows = WEIGHT_CHUNK_ROWS
    n_chunks = wb_ref.shape[0] // rows

    def chunk_copy(c):
        slot = c % WEIGHT_SLOTS
        return pltpu.make_async_copy(w_hbm.at[pl.ds(c * rows, rows)], stage_ref.at[slot], sem_ref.at[slot])

    for c in range(min(WEIGHT_SLOTS - 1, n_chunks)):
        chunk_copy(c).start()
    for c in range(n_chunks):
        if c + WEIGHT_SLOTS - 1 < n_chunks:
            chunk_copy(c + WEIGHT_SLOTS - 1).start()
        chunk_copy(c).wait()
        wb_ref[c * rows:(c + 1) * rows, :] = stage_ref[c % WEIGHT_SLOTS].astype(bf16)


def _weight_scratch(k, n):
    return [pltpu.VMEM((k, n), bf16), pltpu.VMEM((WEIGHT_SLOTS, WEIGHT_CHUNK_ROWS, n), f32),
            pltpu.SemaphoreType.DMA((WEIGHT_SLOTS,))]


def _next_activation(hnew, gnext_ref, next_mode):
    if next_mode == "norm":
        return _rms(hnew, gnext_ref[...]).astype(bf16)
    return hnew.astype(bf16)


def _mm_res_kernel(x_ref, w_hbm, h_ref, gpost_ref, gnext_ref, hout_ref, hn_ref, wb_ref, stage_ref, sem_ref, *,
                   layer, next_mode):
    @pl.when(pl.program_id(0) == 0)
    def _():
        _load_weight_bf16(w_hbm.at[layer], wb_ref, stage_ref, sem_ref)

    mix = _dot(x_ref[...], wb_ref[...])
    hnew = h_ref[...] + _rms(mix, gpost_ref[...])
    hout_ref[...] = hnew
    hn_ref[...] = _next_activation(hnew, gnext_ref, next_mode)


def _ssd_out_kernel(y_ref, z_ref, normw_ref, w_hbm, h_ref, gpost_ref, gnext_ref, hout_ref, hn_ref,
                    wb_ref, stage_ref, sem_ref, *, layer):
    @pl.when(pl.program_id(0) == 0)
    def _():
        _load_weight_bf16(w_hbm.at[layer], wb_ref, stage_ref, sem_ref)

    gw = SSD_GROUP_WIDTH
    parts = []
    for g in range(SSD_N_GROUPS):
        cols = slice(g * gw, (g + 1) * gw)
        z = z_ref[:, cols].astype(f32)
        y = y_ref[:, cols].astype(f32) * _silu(z)
        parts.append(_rms(y, normw_ref[:, cols]).astype(bf16))
    mix = _dot(jnp.concatenate(parts, axis=1), wb_ref[...])
    hnew = h_ref[...] + _rms(mix, gpost_ref[...])
    hout_ref[...] = hnew
    hn_ref[...] = _next_activation(hnew, gnext_ref, "norm")


def _gain_spec(d, layer):
    return pl.BlockSpec((None, 1, d), lambda i: (layer, 0, 0))


def _gains3(g):
    return g.reshape(g.shape[0], 1, g.shape[1])


def _ssd_out(y, z, norm_w, w_stack, w_layer, h, gpost, gnext, layer, tm=256):
    m, k = y.shape
    d = h.shape[1]
    row = lambda i: (i, 0)
    return pl.pallas_call(
        functools.partial(_ssd_out_kernel, layer=w_layer),
        grid=(m // tm,),
        in_specs=[
            pl.BlockSpec((tm, k), row),
            pl.BlockSpec((tm, k), row),
            _gain_spec(k, w_layer),
            pl.BlockSpec(memory_space=pl.ANY),
            pl.BlockSpec((tm, d), row),
            _gain_spec(d, layer),
            _gain_spec(d, layer),
        ],
        out_specs=[pl.BlockSpec((tm, d), row), pl.BlockSpec((tm, d), row)],
        out_shape=[jax.ShapeDtypeStruct((m, d), f32), jax.ShapeDtypeStruct((m, d), bf16)],
        scratch_shapes=_weight_scratch(k, d),
        compiler_params=_params("arbitrary", vmem=VMEM_LIMIT_RESIDENT_BYTES),
        name="ssd_out_proj",
    )(y, z, _gains3(norm_w), w_stack, h, _gains3(gpost), _gains3(gnext))


def _matmul_residual(x, w_stack, w_layer, h, gpost, gnext, layer, next_mode, name, tm=256):
    m, k = x.shape
    d = h.shape[1]
    row = lambda i: (i, 0)
    return pl.pallas_call(
        functools.partial(_mm_res_kernel, layer=w_layer, next_mode=next_mode),
        grid=(m // tm,),
        in_specs=[
            pl.BlockSpec((tm, k), row),
            pl.BlockSpec(memory_space=pl.ANY),
            pl.BlockSpec((tm, d), row),
            _gain_spec(d, layer),
            _gain_spec(d, layer),
        ],
        out_specs=[pl.BlockSpec((tm, d), row), pl.BlockSpec((tm, d), row)],
        out_shape=[jax.ShapeDtypeStruct((m, d), f32), jax.ShapeDtypeStruct((m, d), bf16)],
        scratch_shapes=_weight_scratch(k, d),
        compiler_params=_params("arbitrary", vmem=VMEM_LIMIT_RESIDENT_BYTES),
        name=name,
    )(x, w_stack, h, _gains3(gpost), _gains3(gnext))


def _ple_kernel(hb_ref, h_ref, p_ref, wgate_hbm, wproj_ref, gple_ref, gnext_ref, *refs, layer, next_mode):
    n_out = 1 if next_mode == "none" else 2
    out_refs = refs[:n_out]
    wb_ref, stage_ref, sem_ref = refs[n_out:]

    @pl.when(pl.program_id(0) == 0)
    def _():
        _load_weight_bf16(wgate_hbm.at[layer], wb_ref, stage_ref, sem_ref)

    gate = _dot(hb_ref[...], wb_ref[...])
    pe = _rms(_dot(p_ref[...].astype(bf16), wproj_ref[...].astype(bf16)), gple_ref[...])
    hnew = h_ref[...] + _sigmoid(gate) * pe
    out_refs[0][...] = hnew
    if next_mode != "none":
        out_refs[1][...] = _next_activation(hnew, gnext_ref, next_mode)


def _ple(hb, h, p_stack, wgate_stack, wproj_stack, gple, gnext, layer, next_layer, next_mode, tm=512):
    m, d = h.shape
    pd = p_stack.shape[2]
    row = lambda i: (i, 0)
    out_specs = [pl.BlockSpec((tm, d), row)]
    out_shape = [jax.ShapeDtypeStruct((m, d), f32)]
    if next_mode != "none":
        out_specs.append(pl.BlockSpec((tm, d), row))
        out_shape.append(jax.ShapeDtypeStruct((m, d), bf16))
    return pl.pallas_call(
        functools.partial(_ple_kernel, layer=layer, next_mode=next_mode),
        grid=(m // tm,),
        in_specs=[
            pl.BlockSpec((tm, d), row),
            pl.BlockSpec((tm, d), row),
            pl.BlockSpec((None, tm, pd), lambda i: (layer, i, 0)),
            pl.BlockSpec(memory_space=pl.ANY),
            pl.BlockSpec((None, pd, d), lambda i: (layer, 0, 0), pipeline_mode=pl.Buffered(1)),
            _gain_spec(d, layer),
            _gain_spec(d, next_layer),
        ],
        out_specs=out_specs,
        out_shape=out_shape,
        scratch_shapes=_weight_scratch(d, d),
        compiler_params=_params("arbitrary", vmem=VMEM_LIMIT_RESIDENT_BYTES),
        name="ple",
    )(hb, h, p_stack, wgate_stack, wproj_stack, _gains3(gple), _gains3(gnext))


LOG2E = math.log2(math.e)


def _narrow_weight(w_ref, n_valid):
    row = lax.broadcasted_iota(jnp.int32, w_ref.shape, 0)
    return jnp.where(row < n_valid, w_ref[...], 0.0).astype(bf16)


def _ssd_dt_kernel(*refs, pre_norm):
    if pre_norm:
        x_ref, gain_ref, w_ref, bias_ref, alog_ref, hn_ref, dt_ref, acum_ref, acum_t_ref = refs
        hn = _rms(x_ref[...], gain_ref[...]).astype(bf16)
        hn_ref[...] = hn
    else:
        hn_ref, w_ref, bias_ref, alog_ref, dt_ref, acum_ref, acum_t_ref = refs
        hn = hn_ref[...]
    tm = hn.shape[0]
    q = SSD_CHUNK
    dt = _softplus(_dot_nt(hn, _narrow_weight(w_ref, SSD_N_HEADS)) + bias_ref[...])
    dt_ref[...] = dt
    adt = dt * (-jnp.exp(alog_ref[...]))
    tril = _tri(q, lower=True)
    for c in range(tm // q):
        hi, mid, lo = _split3(adt[c * q:(c + 1) * q, :])
        acum = (_dot(tril, hi) + _dot(tril, mid) + _dot(tril, lo)) * LOG2E
        acum_ref[c * q:(c + 1) * q, :] = acum
        acum_t_ref[:, c * q:(c + 1) * q] = acum.T[:SSD_N_HEADS, :]


def _ssd_dt(x, w_stack, layer, dt_bias, a_log, pre_norm_gain=None, tm=1024):
    m, d = x.shape
    nh = SSD_N_HEADS
    pad_row = lambda v: jnp.pad(v, (0, LANES - nh)).reshape(1, LANES)
    fixed = lambda i: (0, 0)
    row = lambda i: (i, 0)
    pre_norm = pre_norm_gain is not None
    args = [x] + ([pre_norm_gain] if pre_norm else []) + [w_stack, pad_row(dt_bias), pad_row(a_log)]
    in_specs = [pl.BlockSpec((tm, d), row)] + ([pl.BlockSpec((1, d), fixed)] if pre_norm else []) + [
        pl.BlockSpec((None, LANES, d), lambda i: (layer, SSD_ZX_DIM // LANES, 0)),
        pl.BlockSpec((1, LANES), fixed),
        pl.BlockSpec((1, LANES), fixed),
    ]
    out_specs = ([pl.BlockSpec((tm, d), row)] if pre_norm else []) + [
        pl.BlockSpec((tm, LANES), row),
        pl.BlockSpec((tm, LANES), row),
        pl.BlockSpec((nh, tm), lambda i: (0, i)),
    ]
    out_shape = ([jax.ShapeDtypeStruct((m, d), bf16)] if pre_norm else []) + [
        jax.ShapeDtypeStruct((m, LANES), f32),
        jax.ShapeDtypeStruct((m, LANES), f32),
        jax.ShapeDtypeStruct((nh, m), f32),
    ]
    return pl.pallas_call(
        functools.partial(_ssd_dt_kernel, pre_norm=pre_norm),
        grid=(m // tm,),
        in_specs=in_specs,
        out_specs=out_specs,
        out_shape=out_shape,
        compiler_params=_params("parallel"),
        name="ssd_dt",
    )(*args)


def _ssd_scan_kernel(xs_ref, b_ref, c_ref, dt_ref, acum_ref, acum_t_ref, dskip_ref, o_ref, state_ref, expand_ref):
    q = SSD_CHUNK
    gw = SSD_GROUP_WIDTH
    hd = SSD_HEAD_DIM
    nh = SSD_N_HEADS

    @pl.when(pl.program_id(1) == 0)
    def _():
        state_ref[...] = jnp.zeros_like(state_ref)
        k = lax.broadcasted_iota(jnp.int32, expand_ref.shape, 0)
        col = lax.broadcasted_iota(jnp.int32, expand_ref.shape, 1)
        expand_ref[...] = jnp.where((k < 3 * nh) & (k % nh == col // hd), 1.0, 0.0).astype(bf16)

    rows = lax.broadcasted_iota(jnp.int32, (q, q), 0)
    cols = lax.broadcasted_iota(jnp.int32, (q, q), 1)
    tril = rows >= cols
    first_head = cols < hd
    dt_hi, dt_mid, dt_lo = _split3(dt_ref[...])
    dt_parts = jnp.concatenate([dt_hi[:, :nh], dt_mid[:, :nh], dt_lo[:, :nh], jnp.zeros((q, nh), bf16)], axis=1)

    for g in range(SSD_N_GROUPS):
        bg = b_ref[:, g * SSD_D_STATE:(g + 1) * SSD_D_STATE]
        cg = c_ref[:, g * SSD_D_STATE:(g + 1) * SSD_D_STATE]
        cb = _dot_nt(cg, bg)
        xg = xs_ref[:, g * gw:(g + 1) * gw].astype(f32)

        acol = []
        ac_pairs = []
        for i in range(SSD_HEADS_PER_GROUP):
            h = g * SSD_HEADS_PER_GROUP + i
            acol.append(jnp.broadcast_to(acum_ref[:, h:h + 1], (q, q)))
        for pr in range(SSD_HEADS_PER_GROUP // 2):
            ac_pairs.append(jnp.where(first_head, acol[2 * pr], acol[2 * pr + 1]))
        dt_g = _dot(dt_parts, expand_ref[:, g * gw:(g + 1) * gw])
        ac_g = jnp.concatenate(ac_pairs, axis=1)
        a_last = ac_g[q - 1:q, :]

        xdt = xg * dt_g
        st_new = _dot_tn(bg, (xdt * jnp.exp2(a_last - ac_g)).astype(bf16))
        prev = state_ref[g]
        y = _dot(cg, prev.astype(bf16)) * jnp.exp2(ac_g)
        state_ref[g] = prev * jnp.exp2(a_last) + st_new

        y_pairs = []
        for pr in range(SSD_HEADS_PER_GROUP // 2):
            xp = xdt[:, pr * q:(pr + 1) * q]
            rhs = jnp.concatenate(
                [jnp.where(first_head, xp, 0.0).astype(bf16), jnp.where(first_head, 0.0, xp).astype(bf16)], axis=0)
            lhs = []
            for k in range(2):
                i = 2 * pr + k
                h = g * SSD_HEADS_PER_GROUP + i
                seg = acol[i] - acum_t_ref[h:h + 1, :]
                decay = jnp.exp2(jnp.where(tril, seg, -jnp.inf))
                lhs.append((cb * decay).astype(bf16))
            y_pairs.append(_dot(jnp.concatenate(lhs, axis=1), rhs))
        y = y + jnp.concatenate(y_pairs, axis=1) + xg * dskip_ref[:, g * gw:(g + 1) * gw]
        o_ref[:, g * gw:(g + 1) * gw] = y.astype(o_ref.dtype)


def _ssd_scan(xbc, dt, acum, acum_t, d_skip, batch, seq):
    m = xbc.shape[0]
    q = SSD_CHUNK
    nc = seq // q
    di = SSD_D_INNER
    bw = SSD_BC_WIDTH
    chunk = lambda b, c: (b * nc + c, 0)
    fixed = lambda b, c: (0, 0)
    d_cols = jnp.repeat(d_skip, SSD_HEAD_DIM).reshape(1, di)
    return pl.pallas_call(
        _ssd_scan_kernel,
        grid=(batch, nc),
        in_specs=[
            pl.BlockSpec((q, di), chunk),
            pl.BlockSpec((q, bw), lambda b, c: (b * nc + c, di // bw)),
            pl.BlockSpec((q, bw), lambda b, c: (b * nc + c, di // bw + 1)),
            pl.BlockSpec((q, LANES), chunk),
            pl.BlockSpec((q, LANES), chunk),
            pl.BlockSpec((SSD_N_HEADS, q), lambda b, c: (0, b * nc + c)),
            pl.BlockSpec((1, di), fixed),
        ],
        out_specs=pl.BlockSpec((q, di), chunk),
        out_shape=jax.ShapeDtypeStruct((m, di), bf16),
        scratch_shapes=[pltpu.VMEM((SSD_N_GROUPS, SSD_D_STATE, SSD_GROUP_WIDTH), f32),
                        pltpu.VMEM((4 * SSD_N_HEADS, di), bf16)],
        compiler_params=_params("parallel", "arbitrary"),
        name="ssd_scan",
    )(xbc, xbc, xbc, dt, acum, acum_t, d_cols)


def _fox_gate_kernel(hn_ref, w_ref, b_row_ref, csum_ref):
    hn = hn_ref[...]
    seq = hn.shape[0]
    q = LANES
    logf = -_softplus(-(_dot_nt(hn, _narrow_weight(w_ref, FOX_N_HEADS)) + b_row_ref[...]))
    tril = _tri(q, lower=True)
    carry = jnp.zeros((1, LANES), f32)
    for c in range(seq // q):
        hi, mid, lo = _split3(logf[c * q:(c + 1) * q, :])
        cs = _dot(tril, hi) + _dot(tril, mid) + _dot(tril, lo) + carry
        csum_ref[c * q:(c + 1) * q, :] = cs * LOG2E
        carry = cs[q - 1:q, :]


def _fox_gate(hn, w_stack, layer, b_f, batch, seq):
    m, d = hn.shape
    nh = FOX_N_HEADS
    return pl.pallas_call(
        _fox_gate_kernel,
        grid=(batch,),
        in_specs=[
            pl.BlockSpec((seq, d), lambda b: (b, 0)),
            pl.BlockSpec((None, LANES, d), lambda b: (layer, 3 * FOX_WIDTH // LANES, 0)),
            pl.BlockSpec((1, LANES), lambda b: (0, 0)),
        ],
        out_specs=pl.BlockSpec((seq, LANES), lambda b: (b, 0)),
        out_shape=jax.ShapeDtypeStruct((m, LANES), f32),
        compiler_params=_params("parallel"),
        name="fox_gate",
    )(hn, w_stack, jnp.pad(b_f, (0, LANES - nh)).reshape(1, LANES))


ATTN_BLOCK = 512


def _fox_attn_kernel(q_ref, k_ref, v_ref, csum_ref, o_ref, qa_ref, ka_ref, va_ref):
    seq = q_ref.shape[0]
    t = ATTN_BLOCK
    hd = FOX_HEAD_DIM
    head = pl.program_id(1)
    lane = lax.broadcasted_iota(jnp.int32, (seq, LANES), 1)
    c = jnp.sum(jnp.where(lane == head, csum_ref[...], 0.0), axis=1, keepdims=True)
    hi = c.astype(bf16).astype(f32)
    r1 = c - hi
    mid = r1.astype(bf16).astype(f32)
    lo = r1 - mid
    ext_q = jnp.where(lane == 0, hi, jnp.where(lane == 1, mid, jnp.where(lane == 2, lo,
                                                                         jnp.where(lane < 6, 1.0, 0.0))))
    ext_k = jnp.where(lane < 3, 1.0, jnp.where(lane == 3, -hi, jnp.where(lane == 4, -mid,
                                                                         jnp.where(lane == 5, -lo, 0.0))))
    qa_ref[:, :hd] = q_ref[...]
    qa_ref[:, hd:] = ext_q.astype(bf16)
    ka_ref[:, :hd] = k_ref[...]
    ka_ref[:, hd:] = ext_k.astype(bf16)
    va_ref[:, :hd] = v_ref[...]
    va_ref[:, hd:] = jnp.ones((seq, hd), bf16)

    causal = lax.broadcasted_iota(jnp.int32, (t, t), 0) >= lax.broadcasted_iota(jnp.int32, (t, t), 1)
    for qi in range(seq // t):
        q = qa_ref[qi * t:(qi + 1) * t, :]
        for j in range(qi + 1):
            s = _dot_nt(q, ka_ref[j * t:(j + 1) * t, :])
            if j == qi:
                s = jnp.where(causal, s, -jnp.inf)
            v = va_ref[j * t:(j + 1) * t, :]
            m_blk = jnp.max(s, axis=1, keepdims=True)
            if j == 0:
                m_run = m_blk
                acc = _dot(jnp.exp2(s - m_run).astype(bf16), v)
            else:
                m_new = jnp.maximum(m_run, m_blk)
                acc = jnp.exp2(m_run - m_new) * acc + _dot(jnp.exp2(s - m_new).astype(bf16), v)
                m_run = m_new
        o_ref[qi * t:(qi + 1) * t, :] = (acc[:, :hd] * (1.0 / acc[:, hd:hd + 1])).astype(o_ref.dtype)


def _fox_attention(qkv, csum, batch, seq):
    m = qkv.shape[0]
    nh = FOX_N_HEADS
    hd = FOX_HEAD_DIM
    return pl.pallas_call(
        _fox_attn_kernel,
        grid=(batch, nh),
        in_specs=[
            pl.BlockSpec((seq, hd), lambda b, h: (b, h)),
            pl.BlockSpec((seq, hd), lambda b, h: (b, nh + h)),
            pl.BlockSpec((seq, hd), lambda b, h: (b, 2 * nh + h)),
            pl.BlockSpec((seq, LANES), lambda b, h: (b, 0)),
        ],
        out_specs=pl.BlockSpec((seq, hd), lambda b, h: (b, h)),
        out_shape=jax.ShapeDtypeStruct((m, FOX_WIDTH), bf16),
        scratch_shapes=[pltpu.VMEM((seq, 2 * hd), bf16)] * 3,
        compiler_params=_params("parallel", "parallel"),
        name="fox_attn",
    )(qkv, qkv, qkv, csum)


def kernel(x, p, norm_mix_pre, norm_mix_post, norm_ffn_pre, norm_ffn_post, ssd_w_in, ssd_conv_w, ssd_conv_b, ssd_dt_bias, ssd_a_log, ssd_d, ssd_norm_w, ssd_w_out, fox_w_in, fox_b_f, fox_w_out, ffn_w_gate, ffn_w_up, ffn_w_down, ple_w_proj, ple_norm, ple_w_gate):
    batch, seq, d = x.shape
    depth = p.shape[0]
    m = batch * seq
    h = x.reshape(m, d)
    p_stack = p.reshape(depth, m, PLE_DIM)
    ssd_scale = jnp.ones((1, SSD_D_INNER), f32)
    fox_scale = jnp.concatenate(
        [jnp.full((1, FOX_WIDTH), FOX_HEAD_DIM ** -0.5 * LOG2E, f32), jnp.ones((1, 2 * FOX_WIDTH), f32)], axis=1)

    ssd_w_in = jnp.swapaxes(ssd_w_in, 1, 2)
    fox_w_in = jnp.swapaxes(fox_w_in, 1, 2)

    hn = None
    for i in range(depth):
        j = i // 2
        if i % 2 == 0:
            if i == 0:
                hn, dt, acum, acum_t = _ssd_dt(h, ssd_w_in, j, ssd_dt_bias[j], ssd_a_log[j],
                                               pre_norm_gain=norm_mix_pre[0:1])
            else:
                dt, acum, acum_t = _ssd_dt(hn, ssd_w_in, j, ssd_dt_bias[j], ssd_a_log[j])
            z = _in_proj(hn, ssd_w_in, j, SSD_D_INNER, ssd_scale, "ssd_in_proj_z")
            xbc = _in_proj_conv(hn, ssd_w_in, ssd_conv_w, ssd_conv_b, j, SSD_D_INNER, SSD_CONV_DIM, seq)
            y = _ssd_scan(xbc, dt, acum, acum_t, ssd_d[j], batch, seq)
            h, hn = _ssd_out(y, z, ssd_norm_w, ssd_w_out, j, h, norm_mix_post, norm_ffn_pre, i)
        else:
            qkv = _in_proj(hn, fox_w_in, j, 3 * FOX_WIDTH, fox_scale, "fox_in_proj")
            csum = _fox_gate(hn, fox_w_in, j, fox_b_f[j], batch, seq)
            mixed = _fox_attention(qkv, csum, batch, seq)
            h, hn = _matmul_residual(mixed, fox_w_out, j, h, norm_mix_post, norm_ffn_pre, i, "norm", "fox_out_proj",
                                     tm=512)
        act = _ffn_up(hn, ffn_w_gate, ffn_w_up, i)
        h, hb = _matmul_residual(act, ffn_w_down, i, h, norm_ffn_post, norm_ffn_post, i, "cast", "ffn_down")
        last = i == depth - 1
        outs = _ple(hb, h, p_stack, ple_w_gate, ple_w_proj, ple_norm, norm_mix_pre, i, 0 if last else i + 1,
                    "none" if last else "norm")
        h = outs[0]
        if not last:
            hn = outs[1]
    return h.reshape(batch, seq, d)
```

```python
import functools
import math

import jax
import jax.numpy as jnp
from jax import lax
from jax.experimental import pallas as pl
from jax.experimental.pallas import tpu as pltpu

f32 = jnp.float32
bf16 = jnp.bfloat16

D_MODEL = 2048
EPS = 1e-6

SSD_D_INNER = 4096
SSD_HEAD_DIM = 64
SSD_N_HEADS = 64
SSD_N_GROUPS = 8
SSD_HEADS_PER_GROUP = 8
SSD_D_STATE = 128
SSD_CONV_K = 4
SSD_CHUNK = 128
SSD_GROUP_WIDTH = SSD_HEADS_PER_GROUP * SSD_HEAD_DIM
SSD_BC_WIDTH = SSD_N_GROUPS * SSD_D_STATE
SSD_CONV_DIM = SSD_D_INNER + 2 * SSD_BC_WIDTH
SSD_ZX_DIM = SSD_D_INNER + SSD_CONV_DIM

FOX_N_HEADS = 16
FOX_HEAD_DIM = 128
FOX_WIDTH = 2048

FFN_HIDDEN = 5632
PLE_DIM = 256

LANES = 128
VMEM_LIMIT_BYTES = 48 * 1024 * 1024
VMEM_LIMIT_RESIDENT_BYTES = 58 * 1024 * 1024


def _params(*semantics, vmem=VMEM_LIMIT_BYTES):
    return pltpu.CompilerParams(dimension_semantics=semantics, vmem_limit_bytes=vmem)


def _dot(a, b):
    return jnp.dot(a, b, preferred_element_type=f32)


def _dot_nt(a, b):
    return lax.dot_general(a, b, (((1,), (1,)), ((), ())), preferred_element_type=f32)


def _dot_tn(a, b):
    return lax.dot_general(a, b, (((0,), (0,)), ((), ())), preferred_element_type=f32)


def _rms(x, gain):
    return x * lax.rsqrt(jnp.mean(x * x, axis=-1, keepdims=True) + EPS) * gain


def _sigmoid(x):
    return 0.5 + 0.5 * jnp.tanh(0.5 * x)


def _silu(x):
    half = 0.5 * x
    return half + half * jnp.tanh(half)


def _softplus(x):
    return jnp.maximum(x, 0.0) + jnp.log1p(jnp.exp(-jnp.abs(x)))


def _split3(x):
    hi = x.astype(bf16)
    r1 = x - hi.astype(f32)
    mid = r1.astype(bf16)
    lo = (r1 - mid.astype(f32)).astype(bf16)
    return hi, mid, lo


def _tri(n, lower):
    r = lax.broadcasted_iota(jnp.int32, (n, n), 0)
    c = lax.broadcasted_iota(jnp.int32, (n, n), 1)
    return jnp.where((r >= c) if lower else (r <= c), 1.0, 0.0).astype(bf16)


def _in_proj_kernel(x_ref, w_ref, scale_ref, o_ref, wb_ref):
    @pl.when(pl.program_id(1) == 0)
    def _():
        wb_ref[...] = w_ref[...].astype(bf16)

    o_ref[...] = (_dot_nt(x_ref[...], wb_ref[...]) * scale_ref[...]).astype(o_ref.dtype)


def _in_proj(x, w_stack, layer, n_cols, col_scale, name, tm=2048, tn=1024):
    m, k = x.shape
    return pl.pallas_call(
        _in_proj_kernel,
        grid=(n_cols // tn, m // tm),
        in_specs=[
            pl.BlockSpec((tm, k), lambda j, i: (i, 0)),
            pl.BlockSpec((None, tn, k), lambda j, i: (layer, j, 0)),
            pl.BlockSpec((1, tn), lambda j, i: (0, j)),
        ],
        out_specs=pl.BlockSpec((tm, tn), lambda j, i: (i, j)),
        out_shape=jax.ShapeDtypeStruct((m, n_cols), bf16),
        scratch_shapes=[pltpu.VMEM((tn, k), bf16)],
        compiler_params=_params("parallel", "arbitrary", vmem=VMEM_LIMIT_RESIDENT_BYTES),
        name=name,
    )(x, w_stack, col_scale)


CONV_HALO = 8
CONV_ROWS = 256


def _in_proj_conv_kernel(x_ref, w_ref, cw_ref, cb_ref, o_ref, wb_ref, pad_ref, *, tiles_per_seq):
    i = pl.program_id(1)
    tm = x_ref.shape[0]

    @pl.when(i == 0)
    def _():
        wb_ref[...] = w_ref[...].astype(bf16)

    @pl.when(i % tiles_per_seq == 0)
    def _():
        pad_ref[0:CONV_HALO, :] = jnp.zeros((CONV_HALO, pad_ref.shape[1]), f32)

    pad_ref[CONV_HALO:CONV_HALO + tm, :] = _dot_nt(x_ref[...], wb_ref[...])
    for r in range(tm // CONV_ROWS):
        base = CONV_HALO + r * CONV_ROWS
        window = pad_ref[base - CONV_HALO:base + CONV_ROWS, :]
        acc = cb_ref[...] + cw_ref[SSD_CONV_K - 1:SSD_CONV_K, :] * window[CONV_HALO:, :]
        for j in range(SSD_CONV_K - 1):
            shift = SSD_CONV_K - 1 - j
            acc = acc + cw_ref[j:j + 1, :] * pltpu.roll(window, shift, axis=0)[CONV_HALO:, :]
        o_ref[r * CONV_ROWS:(r + 1) * CONV_ROWS, :] = _silu(acc).astype(o_ref.dtype)
    pad_ref[0:CONV_HALO, :] = pad_ref[tm:tm + CONV_HALO, :]


def _in_proj_conv(x, w_stack, conv_w, conv_b, layer, col0, n_cols, seq, tm=1024, tn=1024):
    m, k = x.shape
    ck = conv_w.shape[1]
    return pl.pallas_call(
        functools.partial(_in_proj_conv_kernel, tiles_per_seq=seq // tm),
        grid=(n_cols // tn, m // tm),
        in_specs=[
            pl.BlockSpec((tm, k), lambda j, i: (i, 0)),
            pl.BlockSpec((None, tn, k), lambda j, i: (layer, col0 // tn + j, 0)),
            pl.BlockSpec((None, ck, tn), lambda j, i: (layer, 0, j)),
            pl.BlockSpec((None, 1, tn), lambda j, i: (layer, 0, j)),
        ],
        out_specs=pl.BlockSpec((tm, tn), lambda j, i: (i, j)),
        out_shape=jax.ShapeDtypeStruct((m, n_cols), bf16),
        scratch_shapes=[pltpu.VMEM((tn, k), bf16), pltpu.VMEM((CONV_HALO + tm, tn), f32)],
        compiler_params=_params("parallel", "arbitrary"),
        name="ssd_in_proj_conv",
    )(x, w_stack, conv_w, conv_b.reshape(conv_b.shape[0], 1, n_cols))


def _ffn_up_kernel(x_ref, wg_ref, wu_ref, o_ref, wgb_ref, wub_ref):
    @pl.when(pl.program_id(1) == 0)
    def _():
        wgb_ref[...] = wg_ref[...].astype(bf16)
        wub_ref[...] = wu_ref[...].astype(bf16)

    x = x_ref[...]
    g = _dot(x, wgb_ref[...])
    u = _dot(x, wub_ref[...])
    o_ref[...] = (_silu(g) * u).astype(o_ref.dtype)


def _ffn_up(x, wg_stack, wu_stack, layer, tm=1024, tn=512):
    m, k = x.shape
    n = wg_stack.shape[2]
    w_spec = pl.BlockSpec((None, k, tn), lambda j, i: (layer, 0, j))
    return pl.pallas_call(
        _ffn_up_kernel,
        grid=(n // tn, m // tm),
        in_specs=[pl.BlockSpec((tm, k), lambda j, i: (i, 0)), w_spec, w_spec],
        out_specs=pl.BlockSpec((tm, tn), lambda j, i: (i, j)),
        out_shape=jax.ShapeDtypeStruct((m, n), bf16),
        scratch_shapes=[pltpu.VMEM((k, tn), bf16), pltpu.VMEM((k, tn), bf16)],
        compiler_params=_params("parallel", "arbitrary", vmem=VMEM_LIMIT_RESIDENT_BYTES),
        name="ffn_up",
    )(x, wg_stack, wu_stack)


WEIGHT_CHUNK_ROWS = 256
WEIGHT_SLOTS = 4


def _load_weight_bf16(w_hbm, wb_ref, stage_ref, sem_ref):
    rows = WEIGHT_CHUNK_ROWS
    n_chunks = wb_ref.shape[0] // rows

    def chunk_copy(c):
        slot = c % WEIGHT_SLOTS
        return pltpu.make_async_copy(w_hbm.at[pl.ds(c * rows, rows)], stage_ref.at[slot], sem_ref.at[slot])

    for c in range(min(WEIGHT_SLOTS - 1, n_chunks)):
        chunk_copy(c).start()
    for c in range(n_chunks):
        if c + WEIGHT_SLOTS - 1 < n_chunks:
            chunk_copy(c + WEIGHT_SLOTS - 1).start()
        chunk_copy(c).wait()
        wb_ref[c * rows:(c + 1) * rows, :] = stage_ref[c % WEIGHT_SLOTS].astype(bf16)


def _weight_scratch(k, n):
    return [pltpu.VMEM((k, n), bf16), pltpu.VMEM((WEIGHT_SLOTS, WEIGHT_CHUNK_ROWS, n), f32),
            pltpu.SemaphoreType.DMA((WEIGHT_SLOTS,))]


def _mm_res_kernel(x_ref, w_hbm, h_ref, gpost_ref, *refs, layer, next_norm):
    if next_norm:
        gnext_ref, hout_ref, hn_ref, wb_ref, stage_ref, sem_ref = refs
    else:
        hout_ref, wb_ref, stage_ref, sem_ref = refs

    @pl.when(pl.program_id(0) == 0)
    def _():
        _load_weight_bf16(w_hbm.at[layer], wb_ref, stage_ref, sem_ref)

    mix = _dot(x_ref[...], wb_ref[...])
    hnew = h_ref[...] + _rms(mix, gpost_ref[...])
    hout_ref[...] = hnew
    if next_norm:
        hn_ref[...] = _rms(hnew, gnext_ref[...]).astype(bf16)


def _ssd_out_kernel(y_ref, z_ref, normw_ref, w_hbm, h_ref, gpost_ref, gnext_ref, hout_ref, hn_ref,
                    wb_ref, stage_ref, sem_ref, *, layer):
    @pl.when(pl.program_id(0) == 0)
    def _():
        _load_weight_bf16(w_hbm.at[layer], wb_ref, stage_ref, sem_ref)

    gw = SSD_GROUP_WIDTH
    parts = []
    for g in range(SSD_N_GROUPS):
        cols = slice(g * gw, (g + 1) * gw)
        z = z_ref[:, cols].astype(f32)
        y = y_ref[:, cols].astype(f32) * _silu(z)
        parts.append(_rms(y, normw_ref[:, cols]).astype(bf16))
    mix = _dot(jnp.concatenate(parts, axis=1), wb_ref[...])
    hnew = h_ref[...] + _rms(mix, gpost_ref[...])
    hout_ref[...] = hnew
    hn_ref[...] = _rms(hnew, gnext_ref[...]).astype(bf16)


def _gain_spec(d, layer):
    return pl.BlockSpec((None, 1, d), lambda i: (layer, 0, 0))


def _gains3(g):
    return g.reshape(g.shape[0], 1, g.shape[1])


def _ssd_out(y, z, norm_w, w_stack, w_layer, h, gpost, gnext, layer, tm=256):
    m, k = y.shape
    d = h.shape[1]
    row = lambda i: (i, 0)
    return pl.pallas_call(
        functools.partial(_ssd_out_kernel, layer=w_layer),
        grid=(m // tm,),
        in_specs=[
            pl.BlockSpec((tm, k), row),
            pl.BlockSpec((tm, k), row),
            _gain_spec(k, w_layer),
            pl.BlockSpec(memory_space=pl.ANY),
            pl.BlockSpec((tm, d), row),
            _gain_spec(d, layer),
            _gain_spec(d, layer),
        ],
        out_specs=[pl.BlockSpec((tm, d), row), pl.BlockSpec((tm, d), row)],
        out_shape=[jax.ShapeDtypeStruct((m, d), f32), jax.ShapeDtypeStruct((m, d), bf16)],
        scratch_shapes=_weight_scratch(k, d),
        compiler_params=_params("arbitrary", vmem=VMEM_LIMIT_RESIDENT_BYTES),
        name="ssd_out_proj",
    )(y, z, _gains3(norm_w), w_stack, h, _gains3(gpost), _gains3(gnext))


def _matmul_residual(x, w_stack, w_layer, h, gpost, layer, name, gnext=None, tm=256):
    m, k = x.shape
    d = h.shape[1]
    row = lambda i: (i, 0)
    next_norm = gnext is not None
    args = [x, w_stack, h, _gains3(gpost)] + ([_gains3(gnext)] if next_norm else [])
    in_specs = [
        pl.BlockSpec((tm, k), row),
        pl.BlockSpec(memory_space=pl.ANY),
        pl.BlockSpec((tm, d), row),
        _gain_spec(d, layer),
    ] + ([_gain_spec(d, layer)] if next_norm else [])
    out_specs = [pl.BlockSpec((tm, d), row)] + ([pl.BlockSpec((tm, d), row)] if next_norm else [])
    out_shape = [jax.ShapeDtypeStruct((m, d), f32)] + ([jax.ShapeDtypeStruct((m, d), bf16)] if next_norm else [])
    return pl.pallas_call(
        functools.partial(_mm_res_kernel, layer=w_layer, next_norm=next_norm),
        grid=(m // tm,),
        in_specs=in_specs,
        out_specs=out_specs,
        out_shape=out_shape,
        scratch_shapes=_weight_scratch(k, d),
        compiler_params=_params("arbitrary", vmem=VMEM_LIMIT_RESIDENT_BYTES),
        name=name,
    )(*args)


def _ple_kernel(h_ref, p_ref, wgate_hbm, wproj_ref, gple_ref, *refs, layer, next_norm):
    if next_norm:
        gnext_ref, hout_ref, hn_ref, wb_ref, stage_ref, sem_ref = refs
    else:
        hout_ref, wb_ref, stage_ref, sem_ref = refs

    @pl.when(pl.program_id(0) == 0)
    def _():
        _load_weight_bf16(wgate_hbm.at[layer], wb_ref, stage_ref, sem_ref)

    h = h_ref[...]
    gate = _dot(h.astype(bf16), wb_ref[...])
    pe = _rms(_dot(p_ref[...].astype(bf16), wproj_ref[...].astype(bf16)), gple_ref[...])
    hnew = h + _sigmoid(gate) * pe
    hout_ref[...] = hnew
    if next_norm:
        hn_ref[...] = _rms(hnew, gnext_ref[...]).astype(bf16)


def _ple(h, p_stack, wgate_stack, wproj_stack, gple, layer, gnext=None, next_layer=None, tm=512):
    m, d = h.shape
    pd = p_stack.shape[2]
    row = lambda i: (i, 0)
    next_norm = gnext is not None
    args = [h, p_stack, wgate_stack, wproj_stack, _gains3(gple)] + ([_gains3(gnext)] if next_norm else [])
    in_specs = [
        pl.BlockSpec((tm, d), row),
        pl.BlockSpec((None, tm, pd), lambda i: (layer, i, 0)),
        pl.BlockSpec(memory_space=pl.ANY),
        pl.BlockSpec((None, pd, d), lambda i: (layer, 0, 0), pipeline_mode=pl.Buffered(1)),
        _gain_spec(d, layer),
    ] + ([_gain_spec(d, next_layer)] if next_norm else [])
    out_specs = [pl.BlockSpec((tm, d), row)] + ([pl.BlockSpec((tm, d), row)] if next_norm else [])
    out_shape = [jax.ShapeDtypeStruct((m, d), f32)] + ([jax.ShapeDtypeStruct((m, d), bf16)] if next_norm else [])
    return pl.pallas_call(
        functools.partial(_ple_kernel, layer=layer, next_norm=next_norm),
        grid=(m // tm,),
        in_specs=in_specs,
        out_specs=out_specs,
        out_shape=out_shape,
        scratch_shapes=_weight_scratch(d, d),
        compiler_params=_params("arbitrary", vmem=VMEM_LIMIT_RESIDENT_BYTES),
        name="ple",
    )(*args)


LOG2E = math.log2(math.e)


def _narrow_weight(w_ref, n_valid):
    row = lax.broadcasted_iota(jnp.int32, w_ref.shape, 0)
    return jnp.where(row < n_valid, w_ref[...], 0.0).astype(bf16)


def _ssd_dt_kernel(*refs, pre_norm):
    if pre_norm:
        x_ref, gain_ref, w_ref, bias_ref, alog_ref, hn_ref, dt_ref, acum_ref, acum_t_ref = refs
        hn = _rms(x_ref[...], gain_ref[...]).astype(bf16)
        hn_ref[...] = hn
    else:
        hn_ref, w_ref, bias_ref, alog_ref, dt_ref, acum_ref, acum_t_ref = refs
        hn = hn_ref[...]
    tm = hn.shape[0]
    q = SSD_CHUNK
    dt = _softplus(_dot_nt(hn, _narrow_weight(w_ref, SSD_N_HEADS)) + bias_ref[...])
    dt_ref[...] = dt
    adt = dt * (-jnp.exp(alog_ref[...]))
    tril = _tri(q, lower=True)
    for c in range(tm // q):
        hi, mid, lo = _split3(adt[c * q:(c + 1) * q, :])
        acum = (_dot(tril, hi) + _dot(tril, mid) + _dot(tril, lo)) * LOG2E
        acum_ref[c * q:(c + 1) * q, :] = acum
        acum_t_ref[:, c * q:(c + 1) * q] = acum.T[:SSD_N_HEADS, :]


def _ssd_dt(x, w_stack, layer, dt_bias, a_log, pre_norm_gain=None, tm=1024):
    m, d = x.shape
    nh = SSD_N_HEADS
    pad_row = lambda v: jnp.pad(v, (0, LANES - nh)).reshape(1, LANES)
    fixed = lambda i: (0, 0)
    row = lambda i: (i, 0)
    pre_norm = pre_norm_gain is not None
    args = [x] + ([pre_norm_gain] if pre_norm else []) + [w_stack, pad_row(dt_bias), pad_row(a_log)]
    in_specs = [pl.BlockSpec((tm, d), row)] + ([pl.BlockSpec((1, d), fixed)] if pre_norm else []) + [
        pl.BlockSpec((None, LANES, d), lambda i: (layer, SSD_ZX_DIM // LANES, 0)),
        pl.BlockSpec((1, LANES), fixed),
        pl.BlockSpec((1, LANES), fixed),
    ]
    out_specs = ([pl.BlockSpec((tm, d), row)] if pre_norm else []) + [
        pl.BlockSpec((tm, LANES), row),
        pl.BlockSpec((tm, LANES), row),
        pl.BlockSpec((nh, tm), lambda i: (0, i)),
    ]
    out_shape = ([jax.ShapeDtypeStruct((m, d), bf16)] if pre_norm else []) + [
        jax.ShapeDtypeStruct((m, LANES), f32),
        jax.ShapeDtypeStruct((m, LANES), f32),
        jax.ShapeDtypeStruct((nh, m), f32),
    ]
    return pl.pallas_call(
        functools.partial(_ssd_dt_kernel, pre_norm=pre_norm),
        grid=(m // tm,),
        in_specs=in_specs,
        out_specs=out_specs,
        out_shape=out_shape,
        compiler_params=_params("parallel"),
        name="ssd_dt",
    )(*args)


def _ssd_scan_kernel(xs_ref, b_ref, c_ref, dt_ref, acum_ref, acum_t_ref, dskip_ref, o_ref, state_ref, expand_ref):
    q = SSD_CHUNK
    gw = SSD_GROUP_WIDTH
    hd = SSD_HEAD_DIM
    nh = SSD_N_HEADS

    @pl.when(pl.program_id(1) == 0)
    def _():
        state_ref[...] = jnp.zeros_like(state_ref)
        k = lax.broadcasted_iota(jnp.int32, expand_ref.shape, 0)
        col = lax.broadcasted_iota(jnp.int32, expand_ref.shape, 1)
        expand_ref[...] = jnp.where((k < 3 * nh) & (k % nh == col // hd), 1.0, 0.0).astype(bf16)

    rows = lax.broadcasted_iota(jnp.int32, (q, q), 0)
    cols = lax.broadcasted_iota(jnp.int32, (q, q), 1)
    tril = rows >= cols
    first_head = cols < hd
    dt_hi, dt_mid, dt_lo = _split3(dt_ref[...])
    dt_parts = jnp.concatenate([dt_hi[:, :nh], dt_mid[:, :nh], dt_lo[:, :nh], jnp.zeros((q, nh), bf16)], axis=1)

    for g in range(SSD_N_GROUPS):
        bg = b_ref[:, g * SSD_D_STATE:(g + 1) * SSD_D_STATE]
        cg = c_ref[:, g * SSD_D_STATE:(g + 1) * SSD_D_STATE]
        cb = _dot_nt(cg, bg)
        xg = xs_ref[:, g * gw:(g + 1) * gw].astype(f32)

        acol = []
        ac_pairs = []
        for i in range(SSD_HEADS_PER_GROUP):
            h = g * SSD_HEADS_PER_GROUP + i
            acol.append(jnp.broadcast_to(acum_ref[:, h:h + 1], (q, q)))
        for pr in range(SSD_HEADS_PER_GROUP // 2):
            ac_pairs.append(jnp.where(first_head, acol[2 * pr], acol[2 * pr + 1]))
        dt_g = _dot(dt_parts, expand_ref[:, g * gw:(g + 1) * gw])
        ac_g = jnp.concatenate(ac_pairs, axis=1)
        a_last = ac_g[q - 1:q, :]

        xdt = xg * dt_g
        st_new = _dot_tn(bg, (xdt * jnp.exp2(a_last - ac_g)).astype(bf16))
        prev = state_ref[g]
        y = _dot(cg, prev.astype(bf16)) * jnp.exp2(ac_g)
        state_ref[g] = prev * jnp.exp2(a_last) + st_new

        y_pairs = []
        for pr in range(SSD_HEADS_PER_GROUP // 2):
            xp = xdt[:, pr * q:(pr + 1) * q]
            rhs = jnp.concatenate(
                [jnp.where(first_head, xp, 0.0).astype(bf16), jnp.where(first_head, 0.0, xp).astype(bf16)], axis=0)
            lhs = []
            for k in range(2):
                i = 2 * pr + k
                h = g * SSD_HEADS_PER_GROUP + i
                seg = acol[i] - acum_t_ref[h:h + 1, :]
                decay = jnp.exp2(jnp.where(tril, seg, -jnp.inf))
                lhs.append((cb * decay).astype(bf16))
            y_pairs.append(_dot(jnp.concatenate(lhs, axis=1), rhs))
        y = y + jnp.concatenate(y_pairs, axis=1) + xg * dskip_ref[:, g * gw:(g + 1) * gw]
        o_ref[:, g * gw:(g + 1) * gw] = y.astype(o_ref.dtype)


def _ssd_scan(xbc, dt, acum, acum_t, d_skip, batch, seq):
    m = xbc.shape[0]
    q = SSD_CHUNK
    nc = seq // q
    di = SSD_D_INNER
    bw = SSD_BC_WIDTH
    chunk = lambda b, c: (b * nc + c, 0)
    fixed = lambda b, c: (0, 0)
    d_cols = jnp.repeat(d_skip, SSD_HEAD_DIM).reshape(1, di)
    return pl.pallas_call(
        _ssd_scan_kernel,
        grid=(batch, nc),
        in_specs=[
            pl.BlockSpec((q, di), chunk),
            pl.BlockSpec((q, bw), lambda b, c: (b * nc + c, di // bw)),
            pl.BlockSpec((q, bw), lambda b, c: (b * nc + c, di // bw + 1)),
            pl.BlockSpec((q, LANES), chunk),
            pl.BlockSpec((q, LANES), chunk),
            pl.BlockSpec((SSD_N_HEADS, q), lambda b, c: (0, b * nc + c)),
            pl.BlockSpec((1, di), fixed),
        ],
        out_specs=pl.BlockSpec((q, di), chunk),
        out_shape=jax.ShapeDtypeStruct((m, di), bf16),
        scratch_shapes=[pltpu.VMEM((SSD_N_GROUPS, SSD_D_STATE, SSD_GROUP_WIDTH), f32),
                        pltpu.VMEM((4 * SSD_N_HEADS, di), bf16)],
        compiler_params=_params("parallel", "arbitrary"),
        name="ssd_scan",
    )(xbc, xbc, xbc, dt, acum, acum_t, d_cols)


def _fox_gate_kernel(hn_ref, w_ref, b_row_ref, csum_ref):
    hn = hn_ref[...]
    seq = hn.shape[0]
    q = LANES
    logf = -_softplus(-(_dot_nt(hn, _narrow_weight(w_ref, FOX_N_HEADS)) + b_row_ref[...]))
    tril = _tri(q, lower=True)
    carry = jnp.zeros((1, LANES), f32)
    for c in range(seq // q):
        hi, mid, lo = _split3(logf[c * q:(c + 1) * q, :])
        cs = _dot(tril, hi) + _dot(tril, mid) + _dot(tril, lo) + carry
        csum_ref[c * q:(c + 1) * q, :] = cs * LOG2E
        carry = cs[q - 1:q, :]


def _fox_gate(hn, w_stack, layer, b_f, batch, seq):
    m, d = hn.shape
    nh = FOX_N_HEADS
    return pl.pallas_call(
        _fox_gate_kernel,
        grid=(batch,),
        in_specs=[
            pl.BlockSpec((seq, d), lambda b: (b, 0)),
            pl.BlockSpec((None, LANES, d), lambda b: (layer, 3 * FOX_WIDTH // LANES, 0)),
            pl.BlockSpec((1, LANES), lambda b: (0, 0)),
        ],
        out_specs=pl.BlockSpec((seq, LANES), lambda b: (b, 0)),
        out_shape=jax.ShapeDtypeStruct((m, LANES), f32),
        compiler_params=_params("parallel"),
        name="fox_gate",
    )(hn, w_stack, jnp.pad(b_f, (0, LANES - nh)).reshape(1, LANES))


ATTN_BLOCK = 512


def _fox_attn_kernel(q_ref, k_ref, v_ref, csum_ref, o_ref, qa_ref, ka_ref, va_ref):
    seq = q_ref.shape[0]
    t = ATTN_BLOCK
    hd = FOX_HEAD_DIM
    head = pl.program_id(1)
    lane = lax.broadcasted_iota(jnp.int32, (seq, LANES), 1)
    c = jnp.sum(jnp.where(lane == head, csum_ref[...], 0.0), axis=1, keepdims=True)
    hi = c.astype(bf16).astype(f32)
    r1 = c - hi
    mid = r1.astype(bf16).astype(f32)
    lo = r1 - mid
    ext_q = jnp.where(lane == 0, hi, jnp.where(lane == 1, mid, jnp.where(lane == 2, lo,
                                                                         jnp.where(lane < 6, 1.0, 0.0))))
    ext_k = jnp.where(lane < 3, 1.0, jnp.where(lane == 3, -hi, jnp.where(lane == 4, -mid,
                                                                         jnp.where(lane == 5, -lo, 0.0))))
    qa_ref[:, :hd] = q_ref[...]
    qa_ref[:, hd:] = ext_q.astype(bf16)
    ka_ref[:, :hd] = k_ref[...]
    ka_ref[:, hd:] = ext_k.astype(bf16)
    va_ref[:, :hd] = v_ref[...]
    va_ref[:, hd:] = jnp.ones((seq, hd), bf16)

    causal = lax.broadcasted_iota(jnp.int32, (t, t), 0) >= lax.broadcasted_iota(jnp.int32, (t, t), 1)
    for qi in range(seq // t):
        q = qa_ref[qi * t:(qi + 1) * t, :]
        for j in range(qi + 1):
            s = _dot_nt(q, ka_ref[j * t:(j + 1) * t, :])
            if j == qi:
                s = jnp.where(causal, s, -jnp.inf)
            v = va_ref[j * t:(j + 1) * t, :]
            m_blk = jnp.max(s, axis=1, keepdims=True)
            if j == 0:
                m_run = m_blk
                acc = _dot(jnp.exp2(s - m_run).astype(bf16), v)
            else:
                m_new = jnp.maximum(m_run, m_blk)
                acc = jnp.exp2(m_run - m_new) * acc + _dot(jnp.exp2(s - m_new).astype(bf16), v)
                m_run = m_new
        o_ref[qi * t:(qi + 1) * t, :] = (acc[:, :hd] * (1.0 / acc[:, hd:hd + 1])).astype(o_ref.dtype)


def _fox_attention(qkv, csum, batch, seq):
    m = qkv.shape[0]
    nh = FOX_N_HEADS
    hd = FOX_HEAD_DIM
    return pl.pallas_call(
        _fox_attn_kernel,
        grid=(batch, nh),
        in_specs=[
            pl.BlockSpec((seq, hd), lambda b, h: (b, h)),
            pl.BlockSpec((seq, hd), lambda b, h: (b, nh + h)),
            pl.BlockSpec((seq, hd), lambda b, h: (b, 2 * nh + h)),
            pl.BlockSpec((seq, LANES), lambda b, h: (b, 0)),
        ],
        out_specs=pl.BlockSpec((seq, hd), lambda b, h: (b, h)),
        out_shape=jax.ShapeDtypeStruct((m, FOX_WIDTH), bf16),
        scratch_shapes=[pltpu.VMEM((seq, 2 * hd), bf16)] * 3,
        compiler_params=_params("parallel", "parallel"),
        name="fox_attn",
    )(qkv, qkv, qkv, csum)


def kernel(x, p, norm_mix_pre, norm_mix_post, norm_ffn_pre, norm_ffn_post, ssd_w_in, ssd_conv_w, ssd_conv_b, ssd_dt_bias, ssd_a_log, ssd_d, ssd_norm_w, ssd_w_out, fox_w_in, fox_b_f, fox_w_out, ffn_w_gate, ffn_w_up, ffn_w_down, ple_w_proj, ple_norm, ple_w_gate):
    batch, seq, d = x.shape
    depth = p.shape[0]
    m = batch * seq
    h = x.reshape(m, d)
    p_stack = p.reshape(depth, m, PLE_DIM)
    ssd_scale = jnp.ones((1, SSD_D_INNER), f32)
    fox_scale = jnp.concatenate(
        [jnp.full((1, FOX_WIDTH), FOX_HEAD_DIM ** -0.5 * LOG2E, f32), jnp.ones((1, 2 * FOX_WIDTH), f32)], axis=1)

    ssd_w_in = jnp.swapaxes(ssd_w_in, 1, 2)
    fox_w_in = jnp.swapaxes(fox_w_in, 1, 2)

    hn = None
    for i in range(depth):
        j = i // 2
        if i % 2 == 0:
            if i == 0:
                hn, dt, acum, acum_t = _ssd_dt(h, ssd_w_in, j, ssd_dt_bias[j], ssd_a_log[j],
                                               pre_norm_gain=norm_mix_pre[0:1])
            else:
                dt, acum, acum_t = _ssd_dt(hn, ssd_w_in, j, ssd_dt_bias[j], ssd_a_log[j])
            z = _in_proj(hn, ssd_w_in, j, SSD_D_INNER, ssd_scale, "ssd_in_proj_z")
            xbc = _in_proj_conv(hn, ssd_w_in, ssd_conv_w, ssd_conv_b, j, SSD_D_INNER, SSD_CONV_DIM, seq)
            y = _ssd_scan(xbc, dt, acum, acum_t, ssd_d[j], batch, seq)
            h, hn = _ssd_out(y, z, ssd_norm_w, ssd_w_out, j, h, norm_mix_post, norm_ffn_pre, i)
        else:
            qkv = _in_proj(hn, fox_w_in, j, 3 * FOX_WIDTH, fox_scale, "fox_in_proj")
            csum = _fox_gate(hn, fox_w_in, j, fox_b_f[j], batch, seq)
            mixed = _fox_attention(qkv, csum, batch, seq)
            h, hn = _matmul_residual(mixed, fox_w_out, j, h, norm_mix_post, i, "fox_out_proj", gnext=norm_ffn_pre,
                                     tm=512)
        act = _ffn_up(hn, ffn_w_gate, ffn_w_up, i)
        (h,) = _matmul_residual(act, ffn_w_down, i, h, norm_ffn_post, i, "ffn_down")
        if i == depth - 1:
            (h,) = _ple(h, p_stack, ple_w_gate, ple_w_proj, ple_norm, i)
        else:
            h, hn = _ple(h, p_stack, ple_w_gate, ple_w_proj, ple_norm, i, gnext=norm_mix_pre, next_layer=i + 1)
    return h.reshape(batch, seq, d)
```

```python
import functools
import math

import jax
import jax.numpy as jnp
from jax import lax
from jax.experimental import pallas as pl
from jax.experimental.pallas import tpu as pltpu

f32 = jnp.float32
bf16 = jnp.bfloat16

D_MODEL = 2048
EPS = 1e-6

SSD_D_INNER = 4096
SSD_HEAD_DIM = 64
SSD_N_HEADS = 64
SSD_N_GROUPS = 8
SSD_HEADS_PER_GROUP = 8
SSD_D_STATE = 128
SSD_CONV_K = 4
SSD_CHUNK = 128
SSD_GROUP_WIDTH = SSD_HEADS_PER_GROUP * SSD_HEAD_DIM
SSD_BC_WIDTH = SSD_N_GROUPS * SSD_D_STATE
SSD_CONV_DIM = SSD_D_INNER + 2 * SSD_BC_WIDTH
SSD_ZX_DIM = SSD_D_INNER + SSD_CONV_DIM

FOX_N_HEADS = 16
FOX_HEAD_DIM = 128
FOX_WIDTH = 2048

FFN_HIDDEN = 5632
PLE_DIM = 256

LANES = 128
VMEM_LIMIT_BYTES = 48 * 1024 * 1024
VMEM_LIMIT_RESIDENT_BYTES = 61 * 1024 * 1024


def _params(*semantics, vmem=VMEM_LIMIT_BYTES):
    return pltpu.CompilerParams(dimension_semantics=semantics, vmem_limit_bytes=vmem)


def _dot(a, b):
    return jnp.dot(a, b, preferred_element_type=f32)


def _dot_nt(a, b):
    return lax.dot_general(a, b, (((1,), (1,)), ((), ())), preferred_element_type=f32)


def _dot_tn(a, b):
    return lax.dot_general(a, b, (((0,), (0,)), ((), ())), preferred_element_type=f32)


def _rms(x, gain):
    return x * lax.rsqrt(jnp.mean(x * x, axis=-1, keepdims=True) + EPS) * gain


def _sigmoid(x):
    return 0.5 + 0.5 * jnp.tanh(0.5 * x)


def _silu(x):
    half = 0.5 * x
    return half + half * jnp.tanh(half)


def _softplus(x):
    return jnp.maximum(x, 0.0) + jnp.log1p(jnp.exp(-jnp.abs(x)))


def _split3(x):
    hi = x.astype(bf16)
    r1 = x - hi.astype(f32)
    mid = r1.astype(bf16)
    lo = (r1 - mid.astype(f32)).astype(bf16)
    return hi, mid, lo


def _tri(n, lower):
    r = lax.broadcasted_iota(jnp.int32, (n, n), 0)
    c = lax.broadcasted_iota(jnp.int32, (n, n), 1)
    return jnp.where((r >= c) if lower else (r <= c), 1.0, 0.0).astype(bf16)


def _in_proj_kernel(x_ref, w_ref, scale_ref, o_ref, wb_ref):
    @pl.when(pl.program_id(1) == 0)
    def _():
        wb_ref[...] = w_ref[...].astype(bf16)

    o_ref[...] = (_dot_nt(x_ref[...], wb_ref[...]) * scale_ref[...]).astype(o_ref.dtype)


def _in_proj(x, w_stack, layer, n_cols, col_scale, name, tm=2048, tn=1024):
    m, k = x.shape
    return pl.pallas_call(
        _in_proj_kernel,
        grid=(n_cols // tn, m // tm),
        in_specs=[
            pl.BlockSpec((tm, k), lambda j, i: (i, 0)),
            pl.BlockSpec((None, tn, k), lambda j, i: (layer, j, 0)),
            pl.BlockSpec((1, tn), lambda j, i: (0, j)),
        ],
        out_specs=pl.BlockSpec((tm, tn), lambda j, i: (i, j)),
        out_shape=jax.ShapeDtypeStruct((m, n_cols), bf16),
        scratch_shapes=[pltpu.VMEM((tn, k), bf16)],
        compiler_params=_params("parallel", "arbitrary", vmem=VMEM_LIMIT_RESIDENT_BYTES),
        name=name,
    )(x, w_stack, col_scale)


CONV_HALO = 8
CONV_ROWS = 256


def _in_proj_conv_kernel(x_ref, w_ref, cw_ref, cb_ref, o_ref, wb_ref, pad_ref, *, tiles_per_seq):
    i = pl.program_id(1)
    tm = x_ref.shape[0]

    @pl.when(i == 0)
    def _():
        wb_ref[...] = w_ref[...].astype(bf16)

    @pl.when(i % tiles_per_seq == 0)
    def _():
        pad_ref[0:CONV_HALO, :] = jnp.zeros((CONV_HALO, pad_ref.shape[1]), f32)

    pad_ref[CONV_HALO:CONV_HALO + tm, :] = _dot_nt(x_ref[...], wb_ref[...])
    for r in range(tm // CONV_ROWS):
        base = CONV_HALO + r * CONV_ROWS
        window = pad_ref[base - CONV_HALO:base + CONV_ROWS, :]
        acc = cb_ref[...] + cw_ref[SSD_CONV_K - 1:SSD_CONV_K, :] * window[CONV_HALO:, :]
        for j in range(SSD_CONV_K - 1):
            shift = SSD_CONV_K - 1 - j
            acc = acc + cw_ref[j:j + 1, :] * pltpu.roll(window, shift, axis=0)[CONV_HALO:, :]
        o_ref[r * CONV_ROWS:(r + 1) * CONV_ROWS, :] = _silu(acc).astype(o_ref.dtype)
    pad_ref[0:CONV_HALO, :] = pad_ref[tm:tm + CONV_HALO, :]


def _in_proj_conv(x, w_stack, conv_w, conv_b, layer, col0, n_cols, seq, tm=1024, tn=1024):
    m, k = x.shape
    ck = conv_w.shape[1]
    return pl.pallas_call(
        functools.partial(_in_proj_conv_kernel, tiles_per_seq=seq // tm),
        grid=(n_cols // tn, m // tm),
        in_specs=[
            pl.BlockSpec((tm, k), lambda j, i: (i, 0)),
            pl.BlockSpec((None, tn, k), lambda j, i: (layer, col0 // tn + j, 0)),
            pl.BlockSpec((None, ck, tn), lambda j, i: (layer, 0, j)),
            pl.BlockSpec((None, 1, tn), lambda j, i: (layer, 0, j)),
        ],
        out_specs=pl.BlockSpec((tm, tn), lambda j, i: (i, j)),
        out_shape=jax.ShapeDtypeStruct((m, n_cols), bf16),
        scratch_shapes=[pltpu.VMEM((tn, k), bf16), pltpu.VMEM((CONV_HALO + tm, tn), f32)],
        compiler_params=_params("parallel", "arbitrary"),
        name="ssd_in_proj_conv",
    )(x, w_stack, conv_w, conv_b.reshape(conv_b.shape[0], 1, n_cols))


def _ffn_up_kernel(x_ref, wg_ref, wu_ref, o_ref, wgb_ref, wub_ref):
    @pl.when(pl.program_id(1) == 0)
    def _():
        wgb_ref[...] = wg_ref[...].astype(bf16)
        wub_ref[...] = wu_ref[...].astype(bf16)

    x = x_ref[...]
    g = _dot(x, wgb_ref[...])
    u = _dot(x, wub_ref[...])
    o_ref[...] = (_silu(g) * u).astype(o_ref.dtype)


def _ffn_up(x, wg_stack, wu_stack, layer, tm=1024, tn=512):
    m, k = x.shape
    n = wg_stack.shape[2]
    w_spec = pl.BlockSpec((None, k, tn), lambda j, i: (layer, 0, j))
    return pl.pallas_call(
        _ffn_up_kernel,
        grid=(n // tn, m // tm),
        in_specs=[pl.BlockSpec((tm, k), lambda j, i: (i, 0)), w_spec, w_spec],
        out_specs=pl.BlockSpec((tm, tn), lambda j, i: (i, j)),
        out_shape=jax.ShapeDtypeStruct((m, n), bf16),
        scratch_shapes=[pltpu.VMEM((k, tn), bf16), pltpu.VMEM((k, tn), bf16)],
        compiler_params=_params("parallel", "arbitrary", vmem=VMEM_LIMIT_RESIDENT_BYTES),
        name="ffn_up",
    )(x, wg_stack, wu_stack)


WEIGHT_CHUNK_ROWS = 256
WEIGHT_SLOTS = 4


def _load_weight_bf16(w_hbm, wb_ref, stage_ref, sem_ref):
    rows = WEIGHT_CHUNK_ROWS
    n_chunks = wb_ref.shape[0] // rows

    def chunk_copy(c):
        slot = c % WEIGHT_SLOTS
        return pltpu.make_async_copy(w_hbm.at[pl.ds(c * rows, rows)], stage_ref.at[slot], sem_ref.at[slot])

    for c in range(min(WEIGHT_SLOTS - 1, n_chunks)):
        chunk_copy(c).start()
    for c in range(n_chunks):
        if c + WEIGHT_SLOTS - 1 < n_chunks:
            chunk_copy(c + WEIGHT_SLOTS - 1).start()
        chunk_copy(c).wait()
        wb_ref[c * rows:(c + 1) * rows, :] = stage_ref[c % WEIGHT_SLOTS].astype(bf16)


def _weight_scratch(k, n):
    return [pltpu.VMEM((k, n), bf16), pltpu.VMEM((WEIGHT_SLOTS, WEIGHT_CHUNK_ROWS, n), f32),
            pltpu.SemaphoreType.DMA((WEIGHT_SLOTS,))]


def _mm_res_kernel(x_ref, w_hbm, h_ref, gpost_ref, *refs, layer, next_norm):
    if next_norm:
        gnext_ref, hout_ref, hn_ref, wb_ref, stage_ref, sem_ref = refs
    else:
        hout_ref, wb_ref, stage_ref, sem_ref = refs

    @pl.when(pl.program_id(0) == 0)
    def _():
        _load_weight_bf16(w_hbm.at[layer], wb_ref, stage_ref, sem_ref)

    mix = _dot(x_ref[...], wb_ref[...])
    hnew = h_ref[...] + _rms(mix, gpost_ref[...])
    hout_ref[...] = hnew
    if next_norm:
        hn_ref[...] = _rms(hnew, gnext_ref[...]).astype(bf16)


def _ssd_out_kernel(y_ref, z_ref, normw_ref, w_hbm, h_ref, gpost_ref, gnext_ref, hout_ref, hn_ref,
                    wb_ref, stage_ref, sem_ref, *, layer):
    @pl.when(pl.program_id(0) == 0)
    def _():
        _load_weight_bf16(w_hbm.at[layer], wb_ref, stage_ref, sem_ref)

    gw = SSD_GROUP_WIDTH
    parts = []
    for g in range(SSD_N_GROUPS):
        cols = slice(g * gw, (g + 1) * gw)
        z = z_ref[:, cols].astype(f32)
        y = y_ref[:, cols].astype(f32) * _silu(z)
        parts.append(_rms(y, normw_ref[:, cols]).astype(bf16))
    mix = _dot(jnp.concatenate(parts, axis=1), wb_ref[...])
    hnew = h_ref[...] + _rms(mix, gpost_ref[...])
    hout_ref[...] = hnew
    hn_ref[...] = _rms(hnew, gnext_ref[...]).astype(bf16)


def _gain_spec(d, layer):
    return pl.BlockSpec((None, 1, d), lambda i: (layer, 0, 0))


def _gains3(g):
    return g.reshape(g.shape[0], 1, g.shape[1])


def _ssd_out(y, z, norm_w, w_stack, w_layer, h, gpost, gnext, layer, tm=256):
    m, k = y.shape
    d = h.shape[1]
    row = lambda i: (i, 0)
    return pl.pallas_call(
        functools.partial(_ssd_out_kernel, layer=w_layer),
        grid=(m // tm,),
        in_specs=[
            pl.BlockSpec((tm, k), row),
            pl.BlockSpec((tm, k), row),
            _gain_spec(k, w_layer),
            pl.BlockSpec(memory_space=pl.ANY),
            pl.BlockSpec((tm, d), row),
            _gain_spec(d, layer),
            _gain_spec(d, layer),
        ],
        out_specs=[pl.BlockSpec((tm, d), row), pl.BlockSpec((tm, d), row)],
        out_shape=[jax.ShapeDtypeStruct((m, d), f32), jax.ShapeDtypeStruct((m, d), bf16)],
        scratch_shapes=_weight_scratch(k, d),
        compiler_params=_params("arbitrary", vmem=VMEM_LIMIT_RESIDENT_BYTES),
        name="ssd_out_proj",
    )(y, z, _gains3(norm_w), w_stack, h, _gains3(gpost), _gains3(gnext))


def _matmul_residual(x, w_stack, w_layer, h, gpost, layer, name, gnext=None, tm=256):
    m, k = x.shape
    d = h.shape[1]
    row = lambda i: (i, 0)
    next_norm = gnext is not None
    args = [x, w_stack, h, _gains3(gpost)] + ([_gains3(gnext)] if next_norm else [])
    in_specs = [
        pl.BlockSpec((tm, k), row),
        pl.BlockSpec(memory_space=pl.ANY),
        pl.BlockSpec((tm, d), row),
        _gain_spec(d, layer),
    ] + ([_gain_spec(d, layer)] if next_norm else [])
    out_specs = [pl.BlockSpec((tm, d), row)] + ([pl.BlockSpec((tm, d), row)] if next_norm else [])
    out_shape = [jax.ShapeDtypeStruct((m, d), f32)] + ([jax.ShapeDtypeStruct((m, d), bf16)] if next_norm else [])
    return pl.pallas_call(
        functools.partial(_mm_res_kernel, layer=w_layer, next_norm=next_norm),
        grid=(m // tm,),
        in_specs=in_specs,
        out_specs=out_specs,
        out_shape=out_shape,
        scratch_shapes=_weight_scratch(k, d),
        compiler_params=_params("arbitrary", vmem=VMEM_LIMIT_RESIDENT_BYTES),
        name=name,
    )(*args)


def _ple_kernel(h_ref, p_ref, wgate_hbm, wproj_ref, gple_ref, *refs, layer, next_norm):
    if next_norm:
        gnext_ref, hout_ref, hn_ref, wb_ref, stage_ref, sem_ref = refs
    else:
        hout_ref, wb_ref, stage_ref, sem_ref = refs

    @pl.when(pl.program_id(0) == 0)
    def _():
        _load_weight_bf16(wgate_hbm.at[layer], wb_ref, stage_ref, sem_ref)

    h = h_ref[...]
    gate = _dot(h.astype(bf16), wb_ref[...])
    pe = _rms(_dot(p_ref[...].astype(bf16), wproj_ref[...].astype(bf16)), gple_ref[...])
    hnew = h + _sigmoid(gate) * pe
    hout_ref[...] = hnew
    if next_norm:
        hn_ref[...] = _rms(hnew, gnext_ref[...]).astype(bf16)


def _ple(h, p_stack, wgate_stack, wproj_stack, gple, layer, gnext=None, next_layer=None, tm=512):
    m, d = h.shape
    pd = p_stack.shape[2]
    row = lambda i: (i, 0)
    next_norm = gnext is not None
    args = [h, p_stack, wgate_stack, wproj_stack, _gains3(gple)] + ([_gains3(gnext)] if next_norm else [])
    in_specs = [
        pl.BlockSpec((tm, d), row),
        pl.BlockSpec((None, tm, pd), lambda i: (layer, i, 0)),
        pl.BlockSpec(memory_space=pl.ANY),
        pl.BlockSpec((None, pd, d), lambda i: (layer, 0, 0), pipeline_mode=pl.Buffered(1)),
        _gain_spec(d, layer),
    ] + ([_gain_spec(d, next_layer)] if next_norm else [])
    out_specs = [pl.BlockSpec((tm, d), row)] + ([pl.BlockSpec((tm, d), row)] if next_norm else [])
    out_shape = [jax.ShapeDtypeStruct((m, d), f32)] + ([jax.ShapeDtypeStruct((m, d), bf16)] if next_norm else [])
    return pl.pallas_call(
        functools.partial(_ple_kernel, layer=layer, next_norm=next_norm),
        grid=(m // tm,),
        in_specs=in_specs,
        out_specs=out_specs,
        out_shape=out_shape,
        scratch_shapes=_weight_scratch(d, d),
        compiler_params=_params("arbitrary", vmem=VMEM_LIMIT_RESIDENT_BYTES),
        name="ple",
    )(*args)


LOG2E = math.log2(math.e)


def _narrow_weight(w_ref, n_valid):
    row = lax.broadcasted_iota(jnp.int32, w_ref.shape, 0)
    return jnp.where(row < n_valid, w_ref[...], 0.0).astype(bf16)


def _ssd_dt_kernel(*refs, pre_norm):
    if pre_norm:
        x_ref, gain_ref, w_ref, bias_ref, alog_ref, hn_ref, dt_ref, acum_ref, acum_t_ref = refs
        hn = _rms(x_ref[...], gain_ref[...]).astype(bf16)
        hn_ref[...] = hn
    else:
        hn_ref, w_ref, bias_ref, alog_ref, dt_ref, acum_ref, acum_t_ref = refs
        hn = hn_ref[...]
    tm = hn.shape[0]
    q = SSD_CHUNK
    dt = _softplus(_dot_nt(hn, _narrow_weight(w_ref, SSD_N_HEADS)) + bias_ref[...])
    dt_ref[...] = dt
    adt = dt * (-jnp.exp(alog_ref[...]))
    tril = _tri(q, lower=True)
    for c in range(tm // q):
        hi, mid, lo = _split3(adt[c * q:(c + 1) * q, :])
        acum = (_dot(tril, hi) + _dot(tril, mid) + _dot(tril, lo)) * LOG2E
        acum_ref[c * q:(c + 1) * q, :] = acum
        acum_t_ref[:, c * q:(c + 1) * q] = acum.T[:SSD_N_HEADS, :]


def _ssd_dt(x, w_stack, layer, dt_bias, a_log, pre_norm_gain=None, tm=1024):
    m, d = x.shape
    nh = SSD_N_HEADS
    pad_row = lambda v: jnp.pad(v, (0, LANES - nh)).reshape(1, LANES)
    fixed = lambda i: (0, 0)
    row = lambda i: (i, 0)
    pre_norm = pre_norm_gain is not None
    args = [x] + ([pre_norm_gain] if pre_norm else []) + [w_stack, pad_row(dt_bias), pad_row(a_log)]
    in_specs = [pl.BlockSpec((tm, d), row)] + ([pl.BlockSpec((1, d), fixed)] if pre_norm else []) + [
        pl.BlockSpec((None, LANES, d), lambda i: (layer, SSD_ZX_DIM // LANES, 0)),
        pl.BlockSpec((1, LANES), fixed),
        pl.BlockSpec((1, LANES), fixed),
    ]
    out_specs = ([pl.BlockSpec((tm, d), row)] if pre_norm else []) + [
        pl.BlockSpec((tm, LANES), row),
        pl.BlockSpec((tm, LANES), row),
        pl.BlockSpec((nh, tm), lambda i: (0, i)),
    ]
    out_shape = ([jax.ShapeDtypeStruct((m, d), bf16)] if pre_norm else []) + [
        jax.ShapeDtypeStruct((m, LANES), f32),
        jax.ShapeDtypeStruct((m, LANES), f32),
        jax.ShapeDtypeStruct((nh, m), f32),
    ]
    return pl.pallas_call(
        functools.partial(_ssd_dt_kernel, pre_norm=pre_norm),
        grid=(m // tm,),
        in_specs=in_specs,
        out_specs=out_specs,
        out_shape=out_shape,
        compiler_params=_params("parallel"),
        name="ssd_dt",
    )(*args)


def _ssd_scan_kernel(xs_ref, b_ref, c_ref, dt_ref, acum_ref, acum_t_ref, dskip_ref, o_ref, state_ref, expand_ref):
    q = SSD_CHUNK
    gw = SSD_GROUP_WIDTH
    hd = SSD_HEAD_DIM
    nh = SSD_N_HEADS

    @pl.when(pl.program_id(1) == 0)
    def _():
        state_ref[...] = jnp.zeros_like(state_ref)
        k = lax.broadcasted_iota(jnp.int32, expand_ref.shape, 0)
        col = lax.broadcasted_iota(jnp.int32, expand_ref.shape, 1)
        expand_ref[...] = jnp.where((k < 3 * nh) & (k % nh == col // hd), 1.0, 0.0).astype(bf16)

    rows = lax.broadcasted_iota(jnp.int32, (q, q), 0)
    cols = lax.broadcasted_iota(jnp.int32, (q, q), 1)
    tril = rows >= cols
    first_head = cols < hd
    dt_hi, dt_mid, dt_lo = _split3(dt_ref[...])
    dt_parts = jnp.concatenate([dt_hi[:, :nh], dt_mid[:, :nh], dt_lo[:, :nh], jnp.zeros((q, nh), bf16)], axis=1)

    for g in range(SSD_N_GROUPS):
        bg = b_ref[:, g * SSD_D_STATE:(g + 1) * SSD_D_STATE]
        cg = c_ref[:, g * SSD_D_STATE:(g + 1) * SSD_D_STATE]
        cb = _dot_nt(cg, bg)
        xg = xs_ref[:, g * gw:(g + 1) * gw].astype(f32)

        acol = []
        ac_pairs = []
        for i in range(SSD_HEADS_PER_GROUP):
            h = g * SSD_HEADS_PER_GROUP + i
            acol.append(jnp.broadcast_to(acum_ref[:, h:h + 1], (q, q)))
        for pr in range(SSD_HEADS_PER_GROUP // 2):
            ac_pairs.append(jnp.where(first_head, acol[2 * pr], acol[2 * pr + 1]))
        dt_g = _dot(dt_parts, expand_ref[:, g * gw:(g + 1) * gw])
        ac_g = jnp.concatenate(ac_pairs, axis=1)
        a_last = ac_g[q - 1:q, :]

        xdt = xg * dt_g
        st_new = _dot_tn(bg, (xdt * jnp.exp2(a_last - ac_g)).astype(bf16))
        prev = state_ref[g]
        y = _dot(cg, prev.astype(bf16)) * jnp.exp2(ac_g)
        state_ref[g] = prev * jnp.exp2(a_last) + st_new

        y_pairs = []
        for pr in range(SSD_HEADS_PER_GROUP // 2):
            xp = xdt[:, pr * q:(pr + 1) * q]
            rhs = jnp.concatenate(
                [jnp.where(first_head, xp, 0.0).astype(bf16), jnp.where(first_head, 0.0, xp).astype(bf16)], axis=0)
            lhs = []
            for k in range(2):
                i = 2 * pr + k
                h = g * SSD_HEADS_PER_GROUP + i
                seg = acol[i] - acum_t_ref[h:h + 1, :]
                decay = jnp.exp2(jnp.where(tril, seg, -jnp.inf))
                lhs.append((cb * decay).astype(bf16))
            y_pairs.append(_dot(jnp.concatenate(lhs, axis=1), rhs))
        y = y + jnp.concatenate(y_pairs, axis=1) + xg * dskip_ref[:, g * gw:(g + 1) * gw]
        o_ref[:, g * gw:(g + 1) * gw] = y.astype(o_ref.dtype)


def _ssd_scan(xbc, dt, acum, acum_t, d_skip, batch, seq):
    m = xbc.shape[0]
    q = SSD_CHUNK
    nc = seq // q
    di = SSD_D_INNER
    bw = SSD_BC_WIDTH
    chunk = lambda b, c: (b * nc + c, 0)
    fixed = lambda b, c: (0, 0)
    d_cols = jnp.repeat(d_skip, SSD_HEAD_DIM).reshape(1, di)
    return pl.pallas_call(
        _ssd_scan_kernel,
        grid=(batch, nc),
        in_specs=[
            pl.BlockSpec((q, di), chunk),
            pl.BlockSpec((q, bw), lambda b, c: (b * nc + c, di // bw)),
            pl.BlockSpec((q, bw), lambda b, c: (b * nc + c, di // bw + 1)),
            pl.BlockSpec((q, LANES), chunk),
            pl.BlockSpec((q, LANES), chunk),
            pl.BlockSpec((SSD_N_HEADS, q), lambda b, c: (0, b * nc + c)),
            pl.BlockSpec((1, di), fixed),
        ],
        out_specs=pl.BlockSpec((q, di), chunk),
        out_shape=jax.ShapeDtypeStruct((m, di), bf16),
        scratch_shapes=[pltpu.VMEM((SSD_N_GROUPS, SSD_D_STATE, SSD_GROUP_WIDTH), f32),
                        pltpu.VMEM((4 * SSD_N_HEADS, di), bf16)],
        compiler_params=_params("parallel", "arbitrary"),
        name="ssd_scan",
    )(xbc, xbc, xbc, dt, acum, acum_t, d_cols)


def _fox_gate_kernel(hn_ref, w_ref, b_row_ref, csum_ref):
    hn = hn_ref[...]
    seq = hn.shape[0]
    q = LANES
    logf = -_softplus(-(_dot_nt(hn, _narrow_weight(w_ref, FOX_N_HEADS)) + b_row_ref[...]))
    tril = _tri(q, lower=True)
    carry = jnp.zeros((1, LANES), f32)
    for c in range(seq // q):
        hi, mid, lo = _split3(logf[c * q:(c + 1) * q, :])
        cs = _dot(tril, hi) + _dot(tril, mid) + _dot(tril, lo) + carry
        csum_ref[c * q:(c + 1) * q, :] = cs * LOG2E
        carry = cs[q - 1:q, :]


def _fox_gate(hn, w_stack, layer, b_f, batch, seq):
    m, d = hn.shape
    nh = FOX_N_HEADS
    return pl.pallas_call(
        _fox_gate_kernel,
        grid=(batch,),
        in_specs=[
            pl.BlockSpec((seq, d), lambda b: (b, 0)),
            pl.BlockSpec((None, LANES, d), lambda b: (layer, 3 * FOX_WIDTH // LANES, 0)),
            pl.BlockSpec((1, LANES), lambda b: (0, 0)),
        ],
        out_specs=pl.BlockSpec((seq, LANES), lambda b: (b, 0)),
        out_shape=jax.ShapeDtypeStruct((m, LANES), f32),
        compiler_params=_params("parallel"),
        name="fox_gate",
    )(hn, w_stack, jnp.pad(b_f, (0, LANES - nh)).reshape(1, LANES))


ATTN_BLOCK = 512


def _fox_attn_kernel(q_ref, k_ref, v_ref, csum_ref, o_ref, qa_ref, ka_ref, va_ref):
    seq = q_ref.shape[0]
    t = ATTN_BLOCK
    hd = FOX_HEAD_DIM
    head = pl.program_id(1)
    lane = lax.broadcasted_iota(jnp.int32, (seq, LANES), 1)
    c = jnp.sum(jnp.where(lane == head, csum_ref[...], 0.0), axis=1, keepdims=True)
    hi = c.astype(bf16).astype(f32)
    r1 = c - hi
    mid = r1.astype(bf16).astype(f32)
    lo = r1 - mid
    ext_q = jnp.where(lane == 0, hi, jnp.where(lane == 1, mid, jnp.where(lane == 2, lo,
                                                                         jnp.where(lane < 6, 1.0, 0.0))))
    ext_k = jnp.where(lane < 3, 1.0, jnp.where(lane == 3, -hi, jnp.where(lane == 4, -mid,
                                                                         jnp.where(lane == 5, -lo, 0.0))))
    qa_ref[:, :hd] = q_ref[...]
    qa_ref[:, hd:] = ext_q.astype(bf16)
    ka_ref[:, :hd] = k_ref[...]
    ka_ref[:, hd:] = ext_k.astype(bf16)
    va_ref[:, :hd] = v_ref[...]
    va_ref[:, hd:] = jnp.ones((seq, hd), bf16)

    causal = lax.broadcasted_iota(jnp.int32, (t, t), 0) >= lax.broadcasted_iota(jnp.int32, (t, t), 1)
    for qi in range(seq // t):
        q = qa_ref[qi * t:(qi + 1) * t, :]
        for j in range(qi + 1):
            s = _dot_nt(q, ka_ref[j * t:(j + 1) * t, :])
            if j == qi:
                s = jnp.where(causal, s, -jnp.inf)
            v = va_ref[j * t:(j + 1) * t, :]
            m_blk = jnp.max(s, axis=1, keepdims=True)
            if j == 0:
                m_run = m_blk
                acc = _dot(jnp.exp2(s - m_run).astype(bf16), v)
            else:
                m_new = jnp.maximum(m_run, m_blk)
                acc = jnp.exp2(m_run - m_new) * acc + _dot(jnp.exp2(s - m_new).astype(bf16), v)
                m_run = m_new
        o_ref[qi * t:(qi + 1) * t, :] = (acc[:, :hd] * (1.0 / acc[:, hd:hd + 1])).astype(o_ref.dtype)


def _fox_attention(qkv, csum, batch, seq):
    m = qkv.shape[0]
    nh = FOX_N_HEADS
    hd = FOX_HEAD_DIM
    return pl.pallas_call(
        _fox_attn_kernel,
        grid=(batch, nh),
        in_specs=[
            pl.BlockSpec((seq, hd), lambda b, h: (b, h)),
            pl.BlockSpec((seq, hd), lambda b, h: (b, nh + h)),
            pl.BlockSpec((seq, hd), lambda b, h: (b, 2 * nh + h)),
            pl.BlockSpec((seq, LANES), lambda b, h: (b, 0)),
        ],
        out_specs=pl.BlockSpec((seq, hd), lambda b, h: (b, h)),
        out_shape=jax.ShapeDtypeStruct((m, FOX_WIDTH), bf16),
        scratch_shapes=[pltpu.VMEM((seq, 2 * hd), bf16)] * 3,
        compiler_params=_params("parallel", "parallel"),
        name="fox_attn",
    )(qkv, qkv, qkv, csum)


def kernel(x, p, norm_mix_pre, norm_mix_post, norm_ffn_pre, norm_ffn_post, ssd_w_in, ssd_conv_w, ssd_conv_b, ssd_dt_bias, ssd_a_log, ssd_d, ssd_norm_w, ssd_w_out, fox_w_in, fox_b_f, fox_w_out, ffn_w_gate, ffn_w_up, ffn_w_down, ple_w_proj, ple_norm, ple_w_gate):
    batch, seq, d = x.shape
    depth = p.shape[0]
    m = batch * seq
    h = x.reshape(m, d)
    p_stack = p.reshape(depth, m, PLE_DIM)
    ssd_scale = jnp.ones((1, SSD_D_INNER), f32)
    fox_scale = jnp.concatenate(
        [jnp.full((1, FOX_WIDTH), FOX_HEAD_DIM ** -0.5 * LOG2E, f32), jnp.ones((1, 2 * FOX_WIDTH), f32)], axis=1)

    ssd_w_in = jnp.swapaxes(ssd_w_in, 1, 2)
    fox_w_in = jnp.swapaxes(fox_w_in, 1, 2)

    hn = None
    for i in range(depth):
        j = i // 2
        if i % 2 == 0:
            if i == 0:
                hn, dt, acum, acum_t = _ssd_dt(h, ssd_w_in, j, ssd_dt_bias[j], ssd_a_log[j],
                                               pre_norm_gain=norm_mix_pre[0:1])
            else:
                dt, acum, acum_t = _ssd_dt(hn, ssd_w_in, j, ssd_dt_bias[j], ssd_a_log[j])
            z = _in_proj(hn, ssd_w_in, j, SSD_D_INNER, ssd_scale, "ssd_in_proj_z")
            xbc = _in_proj_conv(hn, ssd_w_in, ssd_conv_w, ssd_conv_b, j, SSD_D_INNER, SSD_CONV_DIM, seq)
            y = _ssd_scan(xbc, dt, acum, acum_t, ssd_d[j], batch, seq)
            h, hn = _ssd_out(y, z, ssd_norm_w, ssd_w_out, j, h, norm_mix_post, norm_ffn_pre, i)
        else:
            qkv = _in_proj(hn, fox_w_in, j, 3 * FOX_WIDTH, fox_scale, "fox_in_proj")
            csum = _fox_gate(hn, fox_w_in, j, fox_b_f[j], batch, seq)
            mixed = _fox_attention(qkv, csum, batch, seq)
            h, hn = _matmul_residual(mixed, fox_w_out, j, h, norm_mix_post, i, "fox_out_proj", gnext=norm_ffn_pre,
                                     tm=512)
        act = _ffn_up(hn, ffn_w_gate, ffn_w_up, i)
        (h,) = _matmul_residual(act, ffn_w_down, i, h, norm_ffn_post, i, "ffn_down", tm=512)
        if i == depth - 1:
            (h,) = _ple(h, p_stack, ple_w_gate, ple_w_proj, ple_norm, i)
        else:
            h, hn = _ple(h, p_stack, ple_w_gate, ple_w_proj, ple_norm, i, gnext=norm_mix_pre, next_layer=i + 1)
    return h.reshape(batch, seq, d)
```

```python
import functools
import math

import jax
import jax.numpy as jnp
from jax import lax
from jax.experimental import pallas as pl
from jax.experimental.pallas import tpu as pltpu

f32 = jnp.float32
bf16 = jnp.bfloat16

EPS = 1e-6

SSD_D_INNER = 4096
SSD_HEAD_DIM = 64
SSD_N_HEADS = 64
SSD_N_GROUPS = 8
SSD_HEADS_PER_GROUP = 8
SSD_D_STATE = 128
SSD_CONV_K = 4
SSD_CHUNK = 128
SSD_GROUP_WIDTH = SSD_HEADS_PER_GROUP * SSD_HEAD_DIM
SSD_BC_WIDTH = SSD_N_GROUPS * SSD_D_STATE
SSD_CONV_DIM = SSD_D_INNER + 2 * SSD_BC_WIDTH
SSD_ZX_DIM = SSD_D_INNER + SSD_CONV_DIM

FOX_N_HEADS = 16
FOX_HEAD_DIM = 128
FOX_WIDTH = 2048

PLE_DIM = 256

LANES = 128
VMEM_LIMIT_BYTES = 48 * 1024 * 1024
VMEM_LIMIT_LARGE_BYTES = 58 * 1024 * 1024


def _params(*semantics, vmem=VMEM_LIMIT_BYTES):
    return pltpu.CompilerParams(dimension_semantics=semantics, vmem_limit_bytes=vmem)


def _dot(a, b):
    return jnp.dot(a, b, preferred_element_type=f32)


def _dot_nt(a, b):
    return lax.dot_general(a, b, (((1,), (1,)), ((), ())), preferred_element_type=f32)


def _dot_tn(a, b):
    return lax.dot_general(a, b, (((0,), (0,)), ((), ())), preferred_element_type=f32)


def _rms(x, gain):
    return x * lax.rsqrt(jnp.mean(x * x, axis=-1, keepdims=True) + EPS) * gain


def _sigmoid(x):
    return 0.5 + 0.5 * jnp.tanh(0.5 * x)


def _silu(x):
    half = 0.5 * x
    return half + half * jnp.tanh(half)


def _softplus(x):
    return jnp.maximum(x, 0.0) + jnp.log1p(jnp.exp(-jnp.abs(x)))


def _split3(x):
    hi = x.astype(bf16)
    r1 = x - hi.astype(f32)
    mid = r1.astype(bf16)
    lo = (r1 - mid.astype(f32)).astype(bf16)
    return hi, mid, lo


def _tri(n, lower):
    r = lax.broadcasted_iota(jnp.int32, (n, n), 0)
    c = lax.broadcasted_iota(jnp.int32, (n, n), 1)
    return jnp.where((r >= c) if lower else (r <= c), 1.0, 0.0).astype(bf16)


def _in_proj_kernel(x_ref, w_ref, scale_ref, o_ref, wb_ref):
    @pl.when(pl.program_id(1) == 0)
    def _():
        wb_ref[...] = w_ref[...].astype(bf16)

    o_ref[...] = (_dot_nt(x_ref[...], wb_ref[...]) * scale_ref[...]).astype(o_ref.dtype)


def _in_proj(x, w_stack, layer, n_cols, col_scale, name, tm=2048, tn=1024):
    m, k = x.shape
    return pl.pallas_call(
        _in_proj_kernel,
        grid=(n_cols // tn, m // tm),
        in_specs=[
            pl.BlockSpec((tm, k), lambda j, i: (i, 0)),
            pl.BlockSpec((None, tn, k), lambda j, i: (layer, j, 0)),
            pl.BlockSpec((1, tn), lambda j, i: (0, j)),
        ],
        out_specs=pl.BlockSpec((tm, tn), lambda j, i: (i, j)),
        out_shape=jax.ShapeDtypeStruct((m, n_cols), bf16),
        scratch_shapes=[pltpu.VMEM((tn, k), bf16)],
        compiler_params=_params("parallel", "arbitrary", vmem=VMEM_LIMIT_LARGE_BYTES),
        name=name,
    )(x, w_stack, col_scale)


CONV_HALO = 8
CONV_ROWS = 256


def _in_proj_conv_kernel(x_ref, w_ref, cw_ref, cb_ref, o_ref, wb_ref, pad_ref, *, tiles_per_seq):
    i = pl.program_id(1)
    tm = x_ref.shape[0]

    @pl.when(i == 0)
    def _():
        wb_ref[...] = w_ref[...].astype(bf16)

    @pl.when(i % tiles_per_seq == 0)
    def _():
        pad_ref[0:CONV_HALO, :] = jnp.zeros((CONV_HALO, pad_ref.shape[1]), f32)

    pad_ref[CONV_HALO:CONV_HALO + tm, :] = _dot_nt(x_ref[...], wb_ref[...])
    for r in range(tm // CONV_ROWS):
        base = CONV_HALO + r * CONV_ROWS
        window = pad_ref[base - CONV_HALO:base + CONV_ROWS, :]
        acc = cb_ref[...] + cw_ref[SSD_CONV_K - 1:SSD_CONV_K, :] * window[CONV_HALO:, :]
        for j in range(SSD_CONV_K - 1):
            shift = SSD_CONV_K - 1 - j
            acc = acc + cw_ref[j:j + 1, :] * pltpu.roll(window, shift, axis=0)[CONV_HALO:, :]
        o_ref[r * CONV_ROWS:(r + 1) * CONV_ROWS, :] = _silu(acc).astype(o_ref.dtype)
    pad_ref[0:CONV_HALO, :] = pad_ref[tm:tm + CONV_HALO, :]


def _in_proj_conv(x, w_stack, conv_w, conv_b, layer, col0, n_cols, seq, tm=1024, tn=1024):
    m, k = x.shape
    ck = conv_w.shape[1]
    return pl.pallas_call(
        functools.partial(_in_proj_conv_kernel, tiles_per_seq=seq // tm),
        grid=(n_cols // tn, m // tm),
        in_specs=[
            pl.BlockSpec((tm, k), lambda j, i: (i, 0)),
            pl.BlockSpec((None, tn, k), lambda j, i: (layer, col0 // tn + j, 0)),
            pl.BlockSpec((None, ck, tn), lambda j, i: (layer, 0, j)),
            pl.BlockSpec((None, 1, tn), lambda j, i: (layer, 0, j)),
        ],
        out_specs=pl.BlockSpec((tm, tn), lambda j, i: (i, j)),
        out_shape=jax.ShapeDtypeStruct((m, n_cols), bf16),
        scratch_shapes=[pltpu.VMEM((tn, k), bf16), pltpu.VMEM((CONV_HALO + tm, tn), f32)],
        compiler_params=_params("parallel", "arbitrary"),
        name="ssd_in_proj_conv",
    )(x, w_stack, conv_w, conv_b.reshape(conv_b.shape[0], 1, n_cols))


def _ffn_up_kernel(x_ref, wg_ref, wu_ref, o_ref, wgb_ref, wub_ref):
    @pl.when(pl.program_id(1) == 0)
    def _():
        wgb_ref[...] = wg_ref[...].astype(bf16)
        wub_ref[...] = wu_ref[...].astype(bf16)

    x = x_ref[...]
    g = _dot(x, wgb_ref[...])
    u = _dot(x, wub_ref[...])
    o_ref[...] = (_silu(g) * u).astype(o_ref.dtype)


def _ffn_up(x, wg_stack, wu_stack, layer, tm=1024, tn=512):
    m, k = x.shape
    n = wg_stack.shape[2]
    w_spec = pl.BlockSpec((None, k, tn), lambda j, i: (layer, 0, j))
    return pl.pallas_call(
        _ffn_up_kernel,
        grid=(n // tn, m // tm),
        in_specs=[pl.BlockSpec((tm, k), lambda j, i: (i, 0)), w_spec, w_spec],
        out_specs=pl.BlockSpec((tm, tn), lambda j, i: (i, j)),
        out_shape=jax.ShapeDtypeStruct((m, n), bf16),
        scratch_shapes=[pltpu.VMEM((k, tn), bf16), pltpu.VMEM((k, tn), bf16)],
        compiler_params=_params("parallel", "arbitrary", vmem=VMEM_LIMIT_LARGE_BYTES),
        name="ffn_up",
    )(x, wg_stack, wu_stack)


WEIGHT_CHUNK_ROWS = 256
WEIGHT_SLOTS = 4


def _load_weight_bf16(w_hbm, wb_ref, stage_ref, sem_ref):
    rows = WEIGHT_CHUNK_ROWS
    n_chunks = wb_ref.shape[0] // rows

    def chunk_copy(c):
        slot = c % WEIGHT_SLOTS
        return pltpu.make_async_copy(w_hbm.at[pl.ds(c * rows, rows)], stage_ref.at[slot], sem_ref.at[slot])

    for c in range(min(WEIGHT_SLOTS - 1, n_chunks)):
        chunk_copy(c).start()
    for c in range(n_chunks):
        if c + WEIGHT_SLOTS - 1 < n_chunks:
            chunk_copy(c + WEIGHT_SLOTS - 1).start()
        chunk_copy(c).wait()
        wb_ref[c * rows:(c + 1) * rows, :] = stage_ref[c % WEIGHT_SLOTS].astype(bf16)


def _weight_scratch(k, n):
    return [pltpu.VMEM((k, n), bf16), pltpu.VMEM((WEIGHT_SLOTS, WEIGHT_CHUNK_ROWS, n), f32),
            pltpu.SemaphoreType.DMA((WEIGHT_SLOTS,))]


def _mm_res_kernel(x_ref, w_hbm, h_ref, gpost_ref, *refs, layer, next_norm):
    if next_norm:
        gnext_ref, hout_ref, hn_ref, wb_ref, stage_ref, sem_ref = refs
    else:
        hout_ref, wb_ref, stage_ref, sem_ref = refs

    @pl.when(pl.program_id(0) == 0)
    def _():
        _load_weight_bf16(w_hbm.at[layer], wb_ref, stage_ref, sem_ref)

    mix = _dot(x_ref[...], wb_ref[...])
    hnew = h_ref[...] + _rms(mix, gpost_ref[...])
    hout_ref[...] = hnew
    if next_norm:
        hn_ref[...] = _rms(hnew, gnext_ref[...]).astype(bf16)


def _ssd_out_kernel(y_ref, z_ref, normw_ref, w_hbm, h_ref, gpost_ref, gnext_ref, hout_ref, hn_ref,
                    wb_ref, stage_ref, sem_ref, *, layer):
    @pl.when(pl.program_id(0) == 0)
    def _():
        _load_weight_bf16(w_hbm.at[layer], wb_ref, stage_ref, sem_ref)

    gw = SSD_GROUP_WIDTH
    parts = []
    for g in range(SSD_N_GROUPS):
        cols = slice(g * gw, (g + 1) * gw)
        z = z_ref[:, cols].astype(f32)
        y = y_ref[:, cols].astype(f32) * _silu(z)
        parts.append(_rms(y, normw_ref[:, cols]).astype(bf16))
    mix = _dot(jnp.concatenate(parts, axis=1), wb_ref[...])
    hnew = h_ref[...] + _rms(mix, gpost_ref[...])
    hout_ref[...] = hnew
    hn_ref[...] = _rms(hnew, gnext_ref[...]).astype(bf16)


def _gain_spec(d, layer):
    return pl.BlockSpec((None, 1, d), lambda i: (layer, 0, 0))


def _gains3(g):
    return g.reshape(g.shape[0], 1, g.shape[1])


def _ssd_out(y, z, norm_w, w_stack, w_layer, h, gpost, gnext, layer, tm=256):
    m, k = y.shape
    d = h.shape[1]
    row = lambda i: (i, 0)
    return pl.pallas_call(
        functools.partial(_ssd_out_kernel, layer=w_layer),
        grid=(m // tm,),
        in_specs=[
            pl.BlockSpec((tm, k), row),
            pl.BlockSpec((tm, k), row),
            _gain_spec(k, w_layer),
            pl.BlockSpec(memory_space=pl.ANY),
            pl.BlockSpec((tm, d), row),
            _gain_spec(d, layer),
            _gain_spec(d, layer),
        ],
        out_specs=[pl.BlockSpec((tm, d), row), pl.BlockSpec((tm, d), row)],
        out_shape=[jax.ShapeDtypeStruct((m, d), f32), jax.ShapeDtypeStruct((m, d), bf16)],
        scratch_shapes=_weight_scratch(k, d),
        compiler_params=_params("arbitrary", vmem=VMEM_LIMIT_LARGE_BYTES),
        name="ssd_out_proj",
    )(y, z, _gains3(norm_w), w_stack, h, _gains3(gpost), _gains3(gnext))


def _matmul_residual(x, w_stack, w_layer, h, gpost, layer, name, gnext=None, tm=256):
    m, k = x.shape
    d = h.shape[1]
    row = lambda i: (i, 0)
    next_norm = gnext is not None
    args = [x, w_stack, h, _gains3(gpost)] + ([_gains3(gnext)] if next_norm else [])
    in_specs = [
        pl.BlockSpec((tm, k), row),
        pl.BlockSpec(memory_space=pl.ANY),
        pl.BlockSpec((tm, d), row),
        _gain_spec(d, layer),
    ] + ([_gain_spec(d, layer)] if next_norm else [])
    out_specs = [pl.BlockSpec((tm, d), row)] + ([pl.BlockSpec((tm, d), row)] if next_norm else [])
    out_shape = [jax.ShapeDtypeStruct((m, d), f32)] + ([jax.ShapeDtypeStruct((m, d), bf16)] if next_norm else [])
    return pl.pallas_call(
        functools.partial(_mm_res_kernel, layer=w_layer, next_norm=next_norm),
        grid=(m // tm,),
        in_specs=in_specs,
        out_specs=out_specs,
        out_shape=out_shape,
        scratch_shapes=_weight_scratch(k, d),
        compiler_params=_params("arbitrary", vmem=VMEM_LIMIT_LARGE_BYTES),
        name=name,
    )(*args)


def _ple_kernel(h_ref, p_ref, wgate_hbm, wproj_ref, gple_ref, *refs, layer, next_norm):
    if next_norm:
        gnext_ref, hout_ref, hn_ref, wb_ref, stage_ref, sem_ref = refs
    else:
        hout_ref, wb_ref, stage_ref, sem_ref = refs

    @pl.when(pl.program_id(0) == 0)
    def _():
        _load_weight_bf16(wgate_hbm.at[layer], wb_ref, stage_ref, sem_ref)

    h = h_ref[...]
    gate = _dot(h.astype(bf16), wb_ref[...])
    pe = _rms(_dot(p_ref[...].astype(bf16), wproj_ref[...].astype(bf16)), gple_ref[...])
    hnew = h + _sigmoid(gate) * pe
    hout_ref[...] = hnew
    if next_norm:
        hn_ref[...] = _rms(hnew, gnext_ref[...]).astype(bf16)


def _ple(h, p_stack, wgate_stack, wproj_stack, gple, layer, gnext=None, next_layer=None, tm=512):
    m, d = h.shape
    pd = p_stack.shape[2]
    row = lambda i: (i, 0)
    next_norm = gnext is not None
    args = [h, p_stack, wgate_stack, wproj_stack, _gains3(gple)] + ([_gains3(gnext)] if next_norm else [])
    in_specs = [
        pl.BlockSpec((tm, d), row),
        pl.BlockSpec((None, tm, pd), lambda i: (layer, i, 0)),
        pl.BlockSpec(memory_space=pl.ANY),
        pl.BlockSpec((None, pd, d), lambda i: (layer, 0, 0), pipeline_mode=pl.Buffered(1)),
        _gain_spec(d, layer),
    ] + ([_gain_spec(d, next_layer)] if next_norm else [])
    out_specs = [pl.BlockSpec((tm, d), row)] + ([pl.BlockSpec((tm, d), row)] if next_norm else [])
    out_shape = [jax.ShapeDtypeStruct((m, d), f32)] + ([jax.ShapeDtypeStruct((m, d), bf16)] if next_norm else [])
    return pl.pallas_call(
        functools.partial(_ple_kernel, layer=layer, next_norm=next_norm),
        grid=(m // tm,),
        in_specs=in_specs,
        out_specs=out_specs,
        out_shape=out_shape,
        scratch_shapes=_weight_scratch(d, d),
        compiler_params=_params("arbitrary", vmem=VMEM_LIMIT_LARGE_BYTES),
        name="ple",
    )(*args)


LOG2E = math.log2(math.e)


def _narrow_weight(w_ref, n_valid):
    row = lax.broadcasted_iota(jnp.int32, w_ref.shape, 0)
    return jnp.where(row < n_valid, w_ref[...], 0.0).astype(bf16)


def _ssd_dt_kernel(*refs, pre_norm):
    if pre_norm:
        x_ref, gain_ref, w_ref, bias_ref, alog_ref, hn_ref, dt_ref, acum_ref, acum_t_ref = refs
        hn = _rms(x_ref[...], gain_ref[...]).astype(bf16)
        hn_ref[...] = hn
    else:
        hn_ref, w_ref, bias_ref, alog_ref, dt_ref, acum_ref, acum_t_ref = refs
        hn = hn_ref[...]
    tm = hn.shape[0]
    q = SSD_CHUNK
    dt = _softplus(_dot_nt(hn, _narrow_weight(w_ref, SSD_N_HEADS)) + bias_ref[...])
    dt_ref[...] = dt
    adt = dt * (-jnp.exp(alog_ref[...]))
    tril = _tri(q, lower=True)
    for c in range(tm // q):
        hi, mid, lo = _split3(adt[c * q:(c + 1) * q, :])
        acum = (_dot(tril, hi) + _dot(tril, mid) + _dot(tril, lo)) * LOG2E
        acum_ref[c * q:(c + 1) * q, :] = acum
        acum_t_ref[:, c * q:(c + 1) * q] = acum.T[:SSD_N_HEADS, :]


def _ssd_dt(x, w_stack, layer, dt_bias, a_log, pre_norm_gain=None, tm=1024):
    m, d = x.shape
    nh = SSD_N_HEADS
    pad_row = lambda v: jnp.pad(v, (0, LANES - nh)).reshape(1, LANES)
    fixed = lambda i: (0, 0)
    row = lambda i: (i, 0)
    pre_norm = pre_norm_gain is not None
    args = [x] + ([pre_norm_gain] if pre_norm else []) + [w_stack, pad_row(dt_bias), pad_row(a_log)]
    in_specs = [pl.BlockSpec((tm, d), row)] + ([pl.BlockSpec((1, d), fixed)] if pre_norm else []) + [
        pl.BlockSpec((None, LANES, d), lambda i: (layer, SSD_ZX_DIM // LANES, 0)),
        pl.BlockSpec((1, LANES), fixed),
        pl.BlockSpec((1, LANES), fixed),
    ]
    out_specs = ([pl.BlockSpec((tm, d), row)] if pre_norm else []) + [
        pl.BlockSpec((tm, LANES), row),
        pl.BlockSpec((tm, LANES), row),
        pl.BlockSpec((nh, tm), lambda i: (0, i)),
    ]
    out_shape = ([jax.ShapeDtypeStruct((m, d), bf16)] if pre_norm else []) + [
        jax.ShapeDtypeStruct((m, LANES), f32),
        jax.ShapeDtypeStruct((m, LANES), f32),
        jax.ShapeDtypeStruct((nh, m), f32),
    ]
    return pl.pallas_call(
        functools.partial(_ssd_dt_kernel, pre_norm=pre_norm),
        grid=(m // tm,),
        in_specs=in_specs,
        out_specs=out_specs,
        out_shape=out_shape,
        compiler_params=_params("parallel"),
        name="ssd_dt",
    )(*args)


def _ssd_scan_kernel(xs_ref, b_ref, c_ref, dt_ref, acum_ref, acum_t_ref, dskip_ref, o_ref, state_ref, expand_ref):
    q = SSD_CHUNK
    gw = SSD_GROUP_WIDTH
    hd = SSD_HEAD_DIM
    nh = SSD_N_HEADS

    @pl.when(pl.program_id(1) == 0)
    def _():
        state_ref[...] = jnp.zeros_like(state_ref)
        k = lax.broadcasted_iota(jnp.int32, expand_ref.shape, 0)
        col = lax.broadcasted_iota(jnp.int32, expand_ref.shape, 1)
        expand_ref[...] = jnp.where((k < 3 * nh) & (k % nh == col // hd), 1.0, 0.0).astype(bf16)

    rows = lax.broadcasted_iota(jnp.int32, (q, q), 0)
    cols = lax.broadcasted_iota(jnp.int32, (q, q), 1)
    tril = rows >= cols
    first_head = cols < hd
    dt_hi, dt_mid, dt_lo = _split3(dt_ref[...])
    dt_parts = jnp.concatenate([dt_hi[:, :nh], dt_mid[:, :nh], dt_lo[:, :nh], jnp.zeros((q, nh), bf16)], axis=1)

    for g in range(SSD_N_GROUPS):
        bg = b_ref[:, g * SSD_D_STATE:(g + 1) * SSD_D_STATE]
        cg = c_ref[:, g * SSD_D_STATE:(g + 1) * SSD_D_STATE]
        cb = _dot_nt(cg, bg)
        xg = xs_ref[:, g * gw:(g + 1) * gw].astype(f32)

        acol = []
        ac_pairs = []
        for i in range(SSD_HEADS_PER_GROUP):
            h = g * SSD_HEADS_PER_GROUP + i
            acol.append(jnp.broadcast_to(acum_ref[:, h:h + 1], (q, q)))
        for pr in range(SSD_HEADS_PER_GROUP // 2):
            ac_pairs.append(jnp.where(first_head, acol[2 * pr], acol[2 * pr + 1]))
        dt_g = _dot(dt_parts, expand_ref[:, g * gw:(g + 1) * gw])
        ac_g = jnp.concatenate(ac_pairs, axis=1)
        a_last = ac_g[q - 1:q, :]

        xdt = xg * dt_g
        st_new = _dot_tn(bg, (xdt * jnp.exp2(a_last - ac_g)).astype(bf16))
        prev = state_ref[g]
        y = _dot(cg, prev.astype(bf16)) * jnp.exp2(ac_g)
        state_ref[g] = prev * jnp.exp2(a_last) + st_new

        y_pairs = []
        for pr in range(SSD_HEADS_PER_GROUP // 2):
            xp = xdt[:, pr * q:(pr + 1) * q]
            rhs = jnp.concatenate(
                [jnp.where(first_head, xp, 0.0).astype(bf16), jnp.where(first_head, 0.0, xp).astype(bf16)], axis=0)
            lhs = []
            for k in range(2):
                i = 2 * pr + k
                h = g * SSD_HEADS_PER_GROUP + i
                seg = acol[i] - acum_t_ref[h:h + 1, :]
                decay = jnp.exp2(jnp.where(tril, seg, -jnp.inf))
                lhs.append((cb * decay).astype(bf16))
            y_pairs.append(_dot(jnp.concatenate(lhs, axis=1), rhs))
        y = y + jnp.concatenate(y_pairs, axis=1) + xg * dskip_ref[:, g * gw:(g + 1) * gw]
        o_ref[:, g * gw:(g + 1) * gw] = y.astype(o_ref.dtype)


def _ssd_scan(xbc, dt, acum, acum_t, d_skip, batch, seq):
    m = xbc.shape[0]
    q = SSD_CHUNK
    nc = seq // q
    di = SSD_D_INNER
    bw = SSD_BC_WIDTH
    chunk = lambda b, c: (b * nc + c, 0)
    fixed = lambda b, c: (0, 0)
    d_cols = jnp.repeat(d_skip, SSD_HEAD_DIM).reshape(1, di)
    return pl.pallas_call(
        _ssd_scan_kernel,
        grid=(batch, nc),
        in_specs=[
            pl.BlockSpec((q, di), chunk),
            pl.BlockSpec((q, bw), lambda b, c: (b * nc + c, di // bw)),
            pl.BlockSpec((q, bw), lambda b, c: (b * nc + c, di // bw + 1)),
            pl.BlockSpec((q, LANES), chunk),
            pl.BlockSpec((q, LANES), chunk),
            pl.BlockSpec((SSD_N_HEADS, q), lambda b, c: (0, b * nc + c)),
            pl.BlockSpec((1, di), fixed),
        ],
        out_specs=pl.BlockSpec((q, di), chunk),
        out_shape=jax.ShapeDtypeStruct((m, di), bf16),
        scratch_shapes=[pltpu.VMEM((SSD_N_GROUPS, SSD_D_STATE, SSD_GROUP_WIDTH), f32),
                        pltpu.VMEM((4 * SSD_N_HEADS, di), bf16)],
        compiler_params=_params("parallel", "arbitrary"),
        name="ssd_scan",
    )(xbc, xbc, xbc, dt, acum, acum_t, d_cols)


def _fox_gate_kernel(hn_ref, w_ref, b_row_ref, csum_ref):
    hn = hn_ref[...]
    seq = hn.shape[0]
    q = LANES
    logf = -_softplus(-(_dot_nt(hn, _narrow_weight(w_ref, FOX_N_HEADS)) + b_row_ref[...]))
    tril = _tri(q, lower=True)
    carry = jnp.zeros((1, LANES), f32)
    for c in range(seq // q):
        hi, mid, lo = _split3(logf[c * q:(c + 1) * q, :])
        cs = _dot(tril, hi) + _dot(tril, mid) + _dot(tril, lo) + carry
        csum_ref[c * q:(c + 1) * q, :] = cs * LOG2E
        carry = cs[q - 1:q, :]


def _fox_gate(hn, w_stack, layer, b_f, batch, seq):
    m, d = hn.shape
    nh = FOX_N_HEADS
    return pl.pallas_call(
        _fox_gate_kernel,
        grid=(batch,),
        in_specs=[
            pl.BlockSpec((seq, d), lambda b: (b, 0)),
            pl.BlockSpec((None, LANES, d), lambda b: (layer, 3 * FOX_WIDTH // LANES, 0)),
            pl.BlockSpec((1, LANES), lambda b: (0, 0)),
        ],
        out_specs=pl.BlockSpec((seq, LANES), lambda b: (b, 0)),
        out_shape=jax.ShapeDtypeStruct((m, LANES), f32),
        compiler_params=_params("parallel"),
        name="fox_gate",
    )(hn, w_stack, jnp.pad(b_f, (0, LANES - nh)).reshape(1, LANES))


ATTN_BLOCK = 512


def _fox_attn_kernel(q_ref, k_ref, v_ref, csum_ref, o_ref, qa_ref, ka_ref, va_ref):
    seq = q_ref.shape[0]
    t = ATTN_BLOCK
    hd = FOX_HEAD_DIM
    head = pl.program_id(1)
    lane = lax.broadcasted_iota(jnp.int32, (seq, LANES), 1)
    c = jnp.sum(jnp.where(lane == head, csum_ref[...], 0.0), axis=1, keepdims=True)
    hi = c.astype(bf16).astype(f32)
    r1 = c - hi
    mid = r1.astype(bf16).astype(f32)
    lo = r1 - mid
    ext_q = jnp.where(lane == 0, hi, jnp.where(lane == 1, mid, jnp.where(lane == 2, lo,
                                                                         jnp.where(lane < 6, 1.0, 0.0))))
    ext_k = jnp.where(lane < 3, 1.0, jnp.where(lane == 3, -hi, jnp.where(lane == 4, -mid,
                                                                         jnp.where(lane == 5, -lo, 0.0))))
    qa_ref[:, :hd] = q_ref[...]
    qa_ref[:, hd:] = ext_q.astype(bf16)
    ka_ref[:, :hd] = k_ref[...]
    ka_ref[:, hd:] = ext_k.astype(bf16)
    va_ref[:, :hd] = v_ref[...]
    va_ref[:, hd:] = jnp.ones((seq, hd), bf16)

    causal = lax.broadcasted_iota(jnp.int32, (t, t), 0) >= lax.broadcasted_iota(jnp.int32, (t, t), 1)
    for qi in range(seq // t):
        q = qa_ref[qi * t:(qi + 1) * t, :]
        for j in range(qi + 1):
            s = _dot_nt(q, ka_ref[j * t:(j + 1) * t, :])
            if j == qi:
                s = jnp.where(causal, s, -jnp.inf)
            v = va_ref[j * t:(j + 1) * t, :]
            m_blk = jnp.max(s, axis=1, keepdims=True)
            if j == 0:
                m_run = m_blk
                acc = _dot(jnp.exp2(s - m_run).astype(bf16), v)
            else:
                m_new = jnp.maximum(m_run, m_blk)
                acc = jnp.exp2(m_run - m_new) * acc + _dot(jnp.exp2(s - m_new).astype(bf16), v)
                m_run = m_new
        o_ref[qi * t:(qi + 1) * t, :] = (acc[:, :hd] * (1.0 / acc[:, hd:hd + 1])).astype(o_ref.dtype)


def _fox_attention(qkv, csum, batch, seq):
    m = qkv.shape[0]
    nh = FOX_N_HEADS
    hd = FOX_HEAD_DIM
    return pl.pallas_call(
        _fox_attn_kernel,
        grid=(batch, nh),
        in_specs=[
            pl.BlockSpec((seq, hd), lambda b, h: (b, h)),
            pl.BlockSpec((seq, hd), lambda b, h: (b, nh + h)),
            pl.BlockSpec((seq, hd), lambda b, h: (b, 2 * nh + h)),
            pl.BlockSpec((seq, LANES), lambda b, h: (b, 0)),
        ],
        out_specs=pl.BlockSpec((seq, hd), lambda b, h: (b, h)),
        out_shape=jax.ShapeDtypeStruct((m, FOX_WIDTH), bf16),
        scratch_shapes=[pltpu.VMEM((seq, 2 * hd), bf16)] * 3,
        compiler_params=_params("parallel", "parallel"),
        name="fox_attn",
    )(qkv, qkv, qkv, csum)


def kernel(x, p, norm_mix_pre, norm_mix_post, norm_ffn_pre, norm_ffn_post, ssd_w_in, ssd_conv_w, ssd_conv_b, ssd_dt_bias, ssd_a_log, ssd_d, ssd_norm_w, ssd_w_out, fox_w_in, fox_b_f, fox_w_out, ffn_w_gate, ffn_w_up, ffn_w_down, ple_w_proj, ple_norm, ple_w_gate):
    batch, seq, d = x.shape
    depth = p.shape[0]
    m = batch * seq
    h = x.reshape(m, d)
    p_stack = p.reshape(depth, m, PLE_DIM)
    ssd_scale = jnp.ones((1, SSD_D_INNER), f32)
    fox_scale = jnp.concatenate(
        [jnp.full((1, FOX_WIDTH), FOX_HEAD_DIM ** -0.5 * LOG2E, f32), jnp.ones((1, 2 * FOX_WIDTH), f32)], axis=1)

    ssd_w_in = jnp.swapaxes(ssd_w_in, 1, 2)
    fox_w_in = jnp.swapaxes(fox_w_in, 1, 2)

    hn = None
    for i in range(depth):
        j = i // 2
        if i % 2 == 0:
            if i == 0:
                hn, dt, acum, acum_t = _ssd_dt(h, ssd_w_in, j, ssd_dt_bias[j], ssd_a_log[j],
                                               pre_norm_gain=norm_mix_pre[0:1])
            else:
                dt, acum, acum_t = _ssd_dt(hn, ssd_w_in, j, ssd_dt_bias[j], ssd_a_log[j])
            z = _in_proj(hn, ssd_w_in, j, SSD_D_INNER, ssd_scale, "ssd_in_proj_z")
            xbc = _in_proj_conv(hn, ssd_w_in, ssd_conv_w, ssd_conv_b, j, SSD_D_INNER, SSD_CONV_DIM, seq)
            y = _ssd_scan(xbc, dt, acum, acum_t, ssd_d[j], batch, seq)
            h, hn = _ssd_out(y, z, ssd_norm_w, ssd_w_out, j, h, norm_mix_post, norm_ffn_pre, i)
        else:
            qkv = _in_proj(hn, fox_w_in, j, 3 * FOX_WIDTH, fox_scale, "fox_in_proj")
            csum = _fox_gate(hn, fox_w_in, j, fox_b_f[j], batch, seq)
            mixed = _fox_attention(qkv, csum, batch, seq)
            h, hn = _matmul_residual(mixed, fox_w_out, j, h, norm_mix_post, i, "fox_out_proj", gnext=norm_ffn_pre,
                                     tm=512)
        act = _ffn_up(hn, ffn_w_gate, ffn_w_up, i)
        (h,) = _matmul_residual(act, ffn_w_down, i, h, norm_ffn_post, i, "ffn_down")
        if i == depth - 1:
            (h,) = _ple(h, p_stack, ple_w_gate, ple_w_proj, ple_norm, i)
        else:
            h, hn = _ple(h, p_stack, ple_w_gate, ple_w_proj, ple_norm, i, gnext=norm_mix_pre, next_layer=i + 1)
    return h.reshape(batch, seq, d)
```

```python
import functools
import math

import jax
import jax.numpy as jnp
from jax import lax
from jax.experimental import pallas as pl
from jax.experimental.pallas import tpu as pltpu

f32 = jnp.float32
bf16 = jnp.bfloat16

D_MODEL = 2048
EPS = 1e-6

SSD_D_INNER = 4096
SSD_HEAD_DIM = 64
SSD_N_HEADS = 64
SSD_N_GROUPS = 8
SSD_HEADS_PER_GROUP = 8
SSD_D_STATE = 128
SSD_CONV_K = 4
SSD_CHUNK = 128
SSD_GROUP_WIDTH = SSD_HEADS_PER_GROUP * SSD_HEAD_DIM
SSD_BC_WIDTH = SSD_N_GROUPS * SSD_D_STATE
SSD_CONV_DIM = SSD_D_INNER + 2 * SSD_BC_WIDTH
SSD_ZX_DIM = SSD_D_INNER + SSD_CONV_DIM

FOX_N_HEADS = 16
FOX_HEAD_DIM = 128
FOX_WIDTH = 2048

FFN_HIDDEN = 5632
PLE_DIM = 256

LANES = 128
VMEM_LIMIT_BYTES = 48 * 1024 * 1024
VMEM_LIMIT_RESIDENT_BYTES = 58 * 1024 * 1024


def _params(*semantics, vmem=VMEM_LIMIT_BYTES):
    return pltpu.CompilerParams(dimension_semantics=semantics, vmem_limit_bytes=vmem)


def _dot(a, b):
    return jnp.dot(a, b, preferred_element_type=f32)


def _dot_nt(a, b):
    return lax.dot_general(a, b, (((1,), (1,)), ((), ())), preferred_element_type=f32)


def _dot_tn(a, b):
    return lax.dot_general(a, b, (((0,), (0,)), ((), ())), preferred_element_type=f32)


def _rms(x, gain):
    return x * lax.rsqrt(jnp.mean(x * x, axis=-1, keepdims=True) + EPS) * gain


def _sigmoid(x):
    return 0.5 + 0.5 * jnp.tanh(0.5 * x)


def _silu(x):
    half = 0.5 * x
    return half + half * jnp.tanh(half)


def _softplus(x):
    return jnp.maximum(x, 0.0) + jnp.log1p(jnp.exp(-jnp.abs(x)))


def _split3(x):
    hi = x.astype(bf16)
    r1 = x - hi.astype(f32)
    mid = r1.astype(bf16)
    lo = (r1 - mid.astype(f32)).astype(bf16)
    return hi, mid, lo


def _tri(n, lower):
    r = lax.broadcasted_iota(jnp.int32, (n, n), 0)
    c = lax.broadcasted_iota(jnp.int32, (n, n), 1)
    return jnp.where((r >= c) if lower else (r <= c), 1.0, 0.0).astype(bf16)


def _in_proj_kernel(x_ref, w_ref, scale_ref, o_ref, wb_ref):
    @pl.when(pl.program_id(1) == 0)
    def _():
        wb_ref[...] = w_ref[...].astype(bf16)

    o_ref[...] = (_dot_nt(x_ref[...], wb_ref[...]) * scale_ref[...]).astype(o_ref.dtype)


def _in_proj(x, w_stack, layer, n_cols, col_scale, name, tm=2048, tn=1024):
    m, k = x.shape
    return pl.pallas_call(
        _in_proj_kernel,
        grid=(n_cols // tn, m // tm),
        in_specs=[
            pl.BlockSpec((tm, k), lambda j, i: (i, 0)),
            pl.BlockSpec((None, tn, k), lambda j, i: (layer, j, 0)),
            pl.BlockSpec((1, tn), lambda j, i: (0, j)),
        ],
        out_specs=pl.BlockSpec((tm, tn), lambda j, i: (i, j)),
        out_shape=jax.ShapeDtypeStruct((m, n_cols), bf16),
        scratch_shapes=[pltpu.VMEM((tn, k), bf16)],
        compiler_params=_params("parallel", "arbitrary", vmem=VMEM_LIMIT_RESIDENT_BYTES),
        name=name,
    )(x, w_stack, col_scale)


CONV_HALO = 8
CONV_ROWS = 256


def _in_proj_conv_kernel(x_ref, w_ref, cw_ref, cb_ref, o_ref, wb_ref, pad_ref, *, tiles_per_seq):
    i = pl.program_id(1)
    tm = x_ref.shape[0]

    @pl.when(i == 0)
    def _():
        wb_ref[...] = w_ref[...].astype(bf16)

    @pl.when(i % tiles_per_seq == 0)
    def _():
        pad_ref[0:CONV_HALO, :] = jnp.zeros((CONV_HALO, pad_ref.shape[1]), f32)

    pad_ref[CONV_HALO:CONV_HALO + tm, :] = _dot_nt(x_ref[...], wb_ref[...])
    for r in range(tm // CONV_ROWS):
        base = CONV_HALO + r * CONV_ROWS
        window = pad_ref[base - CONV_HALO:base + CONV_ROWS, :]
        acc = cb_ref[...] + cw_ref[SSD_CONV_K - 1:SSD_CONV_K, :] * window[CONV_HALO:, :]
        for j in range(SSD_CONV_K - 1):
            shift = SSD_CONV_K - 1 - j
            acc = acc + cw_ref[j:j + 1, :] * pltpu.roll(window, shift, axis=0)[CONV_HALO:, :]
        o_ref[r * CONV_ROWS:(r + 1) * CONV_ROWS, :] = _silu(acc).astype(o_ref.dtype)
    pad_ref[0:CONV_HALO, :] = pad_ref[tm:tm + CONV_HALO, :]


def _in_proj_conv(x, w_stack, conv_w, conv_b, layer, col0, n_cols, seq, tm=1024, tn=1024):
    m, k = x.shape
    ck = conv_w.shape[1]
    return pl.pallas_call(
        functools.partial(_in_proj_conv_kernel, tiles_per_seq=seq // tm),
        grid=(n_cols // tn, m // tm),
        in_specs=[
            pl.BlockSpec((tm, k), lambda j, i: (i, 0)),
            pl.BlockSpec((None, tn, k), lambda j, i: (layer, col0 // tn + j, 0)),
            pl.BlockSpec((None, ck, tn), lambda j, i: (layer, 0, j)),
            pl.BlockSpec((None, 1, tn), lambda j, i: (layer, 0, j)),
        ],
        out_specs=pl.BlockSpec((tm, tn), lambda j, i: (i, j)),
        out_shape=jax.ShapeDtypeStruct((m, n_cols), bf16),
        scratch_shapes=[pltpu.VMEM((tn, k), bf16), pltpu.VMEM((CONV_HALO + tm, tn), f32)],
        compiler_params=_params("parallel", "arbitrary"),
        name="ssd_in_proj_conv",
    )(x, w_stack, conv_w, conv_b.reshape(conv_b.shape[0], 1, n_cols))


FFN_X_SLOTS = 3
FFN_W_SLOTS = 2
FFN_O_SLOTS = 2


def _ffn_up_kernel(x_hbm, wg_hbm, wu_hbm, o_hbm, xbuf, wgf, wuf, wgb, wub, obuf, xsem, wsem, osem, *, layer, tm, tn):
    n_i = x_hbm.shape[0] // tm
    n_j = o_hbm.shape[1] // tn
    total = n_i * n_j

    def x_copy(s):
        row0 = pl.multiple_of((s % n_i) * tm, tm)
        slot = s % FFN_X_SLOTS
        return pltpu.make_async_copy(x_hbm.at[pl.ds(row0, tm)], xbuf.at[slot], xsem.at[slot])

    def w_copies(j):
        col0 = pl.multiple_of(j * tn, tn)
        slot = j % FFN_W_SLOTS
        return (pltpu.make_async_copy(wg_hbm.at[layer, :, pl.ds(col0, tn)], wgf.at[slot], wsem.at[0, slot]),
                pltpu.make_async_copy(wu_hbm.at[layer, :, pl.ds(col0, tn)], wuf.at[slot], wsem.at[1, slot]))

    def o_copy(s):
        row0 = pl.multiple_of((s % n_i) * tm, tm)
        col0 = pl.multiple_of((s // n_i) * tn, tn)
        slot = s % FFN_O_SLOTS
        return pltpu.make_async_copy(obuf.at[slot], o_hbm.at[pl.ds(row0, tm), pl.ds(col0, tn)], osem.at[slot])

    for c in w_copies(0):
        c.start()
    for s in range(FFN_X_SLOTS - 1):
        x_copy(s).start()

    def step(s, carry):
        j = s // n_i

        @pl.when(s % n_i == 0)
        def _():
            for c in w_copies(j):
                c.wait()
            slot = j % FFN_W_SLOTS
            wgb[...] = wgf[slot].astype(bf16)
            wub[...] = wuf[slot].astype(bf16)

            @pl.when(j + 1 < n_j)
            def _():
                for c in w_copies(j + 1):
                    c.start()

        @pl.when(s + FFN_X_SLOTS - 1 < total)
        def _():
            x_copy(s + FFN_X_SLOTS - 1).start()

        x_copy(s).wait()

        @pl.when(s >= FFN_O_SLOTS)
        def _():
            o_copy(s - FFN_O_SLOTS).wait()

        x = xbuf[s % FFN_X_SLOTS]
        g = _dot(x, wgb[...])
        u = _dot(x, wub[...])
        obuf[s % FFN_O_SLOTS] = (_silu(g) * u).astype(bf16)
        o_copy(s).start()
        return carry

    lax.fori_loop(0, total, step, 0)
    for s in range(total - FFN_O_SLOTS, total):
        o_copy(s).wait()


def _ffn_up(x, wg_stack, wu_stack, layer, tm=1024, tn=512):
    m, k = x.shape
    n = wg_stack.shape[2]
    any_spec = pl.BlockSpec(memory_space=pl.ANY)
    return pl.pallas_call(
        functools.partial(_ffn_up_kernel, layer=layer, tm=tm, tn=tn),
        in_specs=[any_spec, any_spec, any_spec],
        out_specs=any_spec,
        out_shape=jax.ShapeDtypeStruct((m, n), bf16),
        scratch_shapes=[
            pltpu.VMEM((FFN_X_SLOTS, tm, k), bf16),
            pltpu.VMEM((FFN_W_SLOTS, k, tn), f32),
            pltpu.VMEM((FFN_W_SLOTS, k, tn), f32),
            pltpu.VMEM((k, tn), bf16),
            pltpu.VMEM((k, tn), bf16),
            pltpu.VMEM((FFN_O_SLOTS, tm, tn), bf16),
            pltpu.SemaphoreType.DMA((FFN_X_SLOTS,)),
            pltpu.SemaphoreType.DMA((2, FFN_W_SLOTS)),
            pltpu.SemaphoreType.DMA((FFN_O_SLOTS,)),
        ],
        compiler_params=pltpu.CompilerParams(vmem_limit_bytes=VMEM_LIMIT_RESIDENT_BYTES),
        name="ffn_up",
    )(x, wg_stack, wu_stack)


WEIGHT_CHUNK_ROWS = 256
WEIGHT_SLOTS = 4


def _load_weight_bf16(w_hbm, wb_ref, stage_ref, sem_ref):
    rows = WEIGHT_CHUNK_ROWS
    n_chunks = wb_ref.shape[0] // rows

    def chunk_copy(c):
        slot = c % WEIGHT_SLOTS
        return pltpu.make_async_copy(w_hbm.at[pl.ds(c * rows, rows)], stage_ref.at[slot], sem_ref.at[slot])

    for c in range(min(WEIGHT_SLOTS - 1, n_chunks)):
        chunk_copy(c).start()
    for c in range(n_chunks):
        if c + WEIGHT_SLOTS - 1 < n_chunks:
            chunk_copy(c + WEIGHT_SLOTS - 1).start()
        chunk_copy(c).wait()
        wb_ref[c * rows:(c + 1) * rows, :] = stage_ref[c % WEIGHT_SLOTS].astype(bf16)


def _weight_scratch(k, n):
    return [pltpu.VMEM((k, n), bf16), pltpu.VMEM((WEIGHT_SLOTS, WEIGHT_CHUNK_ROWS, n), f32),
            pltpu.SemaphoreType.DMA((WEIGHT_SLOTS,))]


def _mm_res_kernel(x_ref, w_hbm, h_ref, gpost_ref, *refs, layer, next_norm):
    if next_norm:
        gnext_ref, hout_ref, hn_ref, wb_ref, stage_ref, sem_ref = refs
    else:
        hout_ref, wb_ref, stage_ref, sem_ref = refs

    @pl.when(pl.program_id(0) == 0)
    def _():
        _load_weight_bf16(w_hbm.at[layer], wb_ref, stage_ref, sem_ref)

    mix = _dot(x_ref[...], wb_ref[...])
    hnew = h_ref[...] + _rms(mix, gpost_ref[...])
    hout_ref[...] = hnew
    if next_norm:
        hn_ref[...] = _rms(hnew, gnext_ref[...]).astype(bf16)


def _ssd_out_kernel(y_ref, z_ref, normw_ref, w_hbm, h_ref, gpost_ref, gnext_ref, hout_ref, hn_ref,
                    wb_ref, stage_ref, sem_ref, *, layer):
    @pl.when(pl.program_id(0) == 0)
    def _():
        _load_weight_bf16(w_hbm.at[layer], wb_ref, stage_ref, sem_ref)

    gw = SSD_GROUP_WIDTH
    parts = []
    for g in range(SSD_N_GROUPS):
        cols = slice(g * gw, (g + 1) * gw)
        z = z_ref[:, cols].astype(f32)
        y = y_ref[:, cols].astype(f32) * _silu(z)
        parts.append(_rms(y, normw_ref[:, cols]).astype(bf16))
    mix = _dot(jnp.concatenate(parts, axis=1), wb_ref[...])
    hnew = h_ref[...] + _rms(mix, gpost_ref[...])
    hout_ref[...] = hnew
    hn_ref[...] = _rms(hnew, gnext_ref[...]).astype(bf16)


def _gain_spec(d, layer):
    return pl.BlockSpec((None, 1, d), lambda i: (layer, 0, 0))


def _gains3(g):
    return g.reshape(g.shape[0], 1, g.shape[1])


def _ssd_out(y, z, norm_w, w_stack, w_layer, h, gpost, gnext, layer, tm=256):
    m, k = y.shape
    d = h.shape[1]
    row = lambda i: (i, 0)
    return pl.pallas_call(
        functools.partial(_ssd_out_kernel, layer=w_layer),
        grid=(m // tm,),
        in_specs=[
            pl.BlockSpec((tm, k), row),
            pl.BlockSpec((tm, k), row),
            _gain_spec(k, w_layer),
            pl.BlockSpec(memory_space=pl.ANY),
            pl.BlockSpec((tm, d), row),
            _gain_spec(d, layer),
            _gain_spec(d, layer),
        ],
        out_specs=[pl.BlockSpec((tm, d), row), pl.BlockSpec((tm, d), row)],
        out_shape=[jax.ShapeDtypeStruct((m, d), f32), jax.ShapeDtypeStruct((m, d), bf16)],
        scratch_shapes=_weight_scratch(k, d),
        compiler_params=_params("arbitrary", vmem=VMEM_LIMIT_RESIDENT_BYTES),
        name="ssd_out_proj",
    )(y, z, _gains3(norm_w), w_stack, h, _gains3(gpost), _gains3(gnext))


def _matmul_residual(x, w_stack, w_layer, h, gpost, layer, name, gnext=None, tm=256):
    m, k = x.shape
    d = h.shape[1]
    row = lambda i: (i, 0)
    next_norm = gnext is not None
    args = [x, w_stack, h, _gains3(gpost)] + ([_gains3(gnext)] if next_norm else [])
    in_specs = [
        pl.BlockSpec((tm, k), row),
        pl.BlockSpec(memory_space=pl.ANY),
        pl.BlockSpec((tm, d), row),
        _gain_spec(d, layer),
    ] + ([_gain_spec(d, layer)] if next_norm else [])
    out_specs = [pl.BlockSpec((tm, d), row)] + ([pl.BlockSpec((tm, d), row)] if next_norm else [])
    out_shape = [jax.ShapeDtypeStruct((m, d), f32)] + ([jax.ShapeDtypeStruct((m, d), bf16)] if next_norm else [])
    return pl.pallas_call(
        functools.partial(_mm_res_kernel, layer=w_layer, next_norm=next_norm),
        grid=(m // tm,),
        in_specs=in_specs,
        out_specs=out_specs,
        out_shape=out_shape,
        scratch_shapes=_weight_scratch(k, d),
        compiler_params=_params("arbitrary", vmem=VMEM_LIMIT_RESIDENT_BYTES),
        name=name,
    )(*args)


def _ple_kernel(h_ref, p_ref, wgate_hbm, wproj_ref, gple_ref, *refs, layer, next_norm):
    if next_norm:
        gnext_ref, hout_ref, hn_ref, wb_ref, stage_ref, sem_ref = refs
    else:
        hout_ref, wb_ref, stage_ref, sem_ref = refs

    @pl.when(pl.program_id(0) == 0)
    def _():
        _load_weight_bf16(wgate_hbm.at[layer], wb_ref, stage_ref, sem_ref)

    h = h_ref[...]
    gate = _dot(h.astype(bf16), wb_ref[...])
    pe = _rms(_dot(p_ref[...].astype(bf16), wproj_ref[...].astype(bf16)), gple_ref[...])
    hnew = h + _sigmoid(gate) * pe
    hout_ref[...] = hnew
    if next_norm:
        hn_ref[...] = _rms(hnew, gnext_ref[...]).astype(bf16)


def _ple(h, p_stack, wgate_stack, wproj_stack, gple, layer, gnext=None, next_layer=None, tm=512):
    m, d = h.shape
    pd = p_stack.shape[2]
    row = lambda i: (i, 0)
    next_norm = gnext is not None
    args = [h, p_stack, wgate_stack, wproj_stack, _gains3(gple)] + ([_gains3(gnext)] if next_norm else [])
    in_specs = [
        pl.BlockSpec((tm, d), row),
        pl.BlockSpec((None, tm, pd), lambda i: (layer, i, 0)),
        pl.BlockSpec(memory_space=pl.ANY),
        pl.BlockSpec((None, pd, d), lambda i: (layer, 0, 0), pipeline_mode=pl.Buffered(1)),
        _gain_spec(d, layer),
    ] + ([_gain_spec(d, next_layer)] if next_norm else [])
    out_specs = [pl.BlockSpec((tm, d), row)] + ([pl.BlockSpec((tm, d), row)] if next_norm else [])
    out_shape = [jax.ShapeDtypeStruct((m, d), f32)] + ([jax.ShapeDtypeStruct((m, d), bf16)] if next_norm else [])
    return pl.pallas_call(
        functools.partial(_ple_kernel, layer=layer, next_norm=next_norm),
        grid=(m // tm,),
        in_specs=in_specs,
        out_specs=out_specs,
        out_shape=out_shape,
        scratch_shapes=_weight_scratch(d, d),
        compiler_params=_params("arbitrary", vmem=VMEM_LIMIT_RESIDENT_BYTES),
        name="ple",
    )(*args)


LOG2E = math.log2(math.e)


def _narrow_weight(w_ref, n_valid):
    row = lax.broadcasted_iota(jnp.int32, w_ref.shape, 0)
    return jnp.where(row < n_valid, w_ref[...], 0.0).astype(bf16)


def _ssd_dt_kernel(*refs, pre_norm):
    if pre_norm:
        x_ref, gain_ref, w_ref, bias_ref, alog_ref, hn_ref, dt_ref, acum_ref, acum_t_ref = refs
        hn = _rms(x_ref[...], gain_ref[...]).astype(bf16)
        hn_ref[...] = hn
    else:
        hn_ref, w_ref, bias_ref, alog_ref, dt_ref, acum_ref, acum_t_ref = refs
        hn = hn_ref[...]
    tm = hn.shape[0]
    q = SSD_CHUNK
    dt = _softplus(_dot_nt(hn, _narrow_weight(w_ref, SSD_N_HEADS)) + bias_ref[...])
    dt_ref[...] = dt
    adt = dt * (-jnp.exp(alog_ref[...]))
    tril = _tri(q, lower=True)
    for c in range(tm // q):
        hi, mid, lo = _split3(adt[c * q:(c + 1) * q, :])
        acum = (_dot(tril, hi) + _dot(tril, mid) + _dot(tril, lo)) * LOG2E
        acum_ref[c * q:(c + 1) * q, :] = acum
        acum_t_ref[:, c * q:(c + 1) * q] = acum.T[:SSD_N_HEADS, :]


def _ssd_dt(x, w_stack, layer, dt_bias, a_log, pre_norm_gain=None, tm=1024):
    m, d = x.shape
    nh = SSD_N_HEADS
    pad_row = lambda v: jnp.pad(v, (0, LANES - nh)).reshape(1, LANES)
    fixed = lambda i: (0, 0)
    row = lambda i: (i, 0)
    pre_norm = pre_norm_gain is not None
    args = [x] + ([pre_norm_gain] if pre_norm else []) + [w_stack, pad_row(dt_bias), pad_row(a_log)]
    in_specs = [pl.BlockSpec((tm, d), row)] + ([pl.BlockSpec((1, d), fixed)] if pre_norm else []) + [
        pl.BlockSpec((None, LANES, d), lambda i: (layer, SSD_ZX_DIM // LANES, 0)),
        pl.BlockSpec((1, LANES), fixed),
        pl.BlockSpec((1, LANES), fixed),
    ]
    out_specs = ([pl.BlockSpec((tm, d), row)] if pre_norm else []) + [
        pl.BlockSpec((tm, LANES), row),
        pl.BlockSpec((tm, LANES), row),
        pl.BlockSpec((nh, tm), lambda i: (0, i)),
    ]
    out_shape = ([jax.ShapeDtypeStruct((m, d), bf16)] if pre_norm else []) + [
        jax.ShapeDtypeStruct((m, LANES), f32),
        jax.ShapeDtypeStruct((m, LANES), f32),
        jax.ShapeDtypeStruct((nh, m), f32),
    ]
    return pl.pallas_call(
        functools.partial(_ssd_dt_kernel, pre_norm=pre_norm),
        grid=(m // tm,),
        in_specs=in_specs,
        out_specs=out_specs,
        out_shape=out_shape,
        compiler_params=_params("parallel"),
        name="ssd_dt",
    )(*args)


def _ssd_scan_kernel(xs_ref, b_ref, c_ref, dt_ref, acum_ref, acum_t_ref, dskip_ref, o_ref, state_ref, expand_ref):
    q = SSD_CHUNK
    gw = SSD_GROUP_WIDTH
    hd = SSD_HEAD_DIM
    nh = SSD_N_HEADS

    @pl.when(pl.program_id(1) == 0)
    def _():
        state_ref[...] = jnp.zeros_like(state_ref)
        k = lax.broadcasted_iota(jnp.int32, expand_ref.shape, 0)
        col = lax.broadcasted_iota(jnp.int32, expand_ref.shape, 1)
        expand_ref[...] = jnp.where((k < 3 * nh) & (k % nh == col // hd), 1.0, 0.0).astype(bf16)

    rows = lax.broadcasted_iota(jnp.int32, (q, q), 0)
    cols = lax.broadcasted_iota(jnp.int32, (q, q), 1)
    tril = rows >= cols
    first_head = cols < hd
    dt_hi, dt_mid, dt_lo = _split3(dt_ref[...])
    dt_parts = jnp.concatenate([dt_hi[:, :nh], dt_mid[:, :nh], dt_lo[:, :nh], jnp.zeros((q, nh), bf16)], axis=1)

    for g in range(SSD_N_GROUPS):
        bg = b_ref[:, g * SSD_D_STATE:(g + 1) * SSD_D_STATE]
        cg = c_ref[:, g * SSD_D_STATE:(g + 1) * SSD_D_STATE]
        cb = _dot_nt(cg, bg)
        xg = xs_ref[:, g * gw:(g + 1) * gw].astype(f32)

        acol = []
        ac_pairs = []
        for i in range(SSD_HEADS_PER_GROUP):
            h = g * SSD_HEADS_PER_GROUP + i
            acol.append(jnp.broadcast_to(acum_ref[:, h:h + 1], (q, q)))
        for pr in range(SSD_HEADS_PER_GROUP // 2):
            ac_pairs.append(jnp.where(first_head, acol[2 * pr], acol[2 * pr + 1]))
        dt_g = _dot(dt_parts, expand_ref[:, g * gw:(g + 1) * gw])
        ac_g = jnp.concatenate(ac_pairs, axis=1)
        a_last = ac_g[q - 1:q, :]

        xdt = xg * dt_g
        st_new = _dot_tn(bg, (xdt * jnp.exp2(a_last - ac_g)).astype(bf16))
        prev = state_ref[g]
        y = _dot(cg, prev.astype(bf16)) * jnp.exp2(ac_g)
        state_ref[g] = prev * jnp.exp2(a_last) + st_new

        y_pairs = []
        for pr in range(SSD_HEADS_PER_GROUP // 2):
            xp = xdt[:, pr * q:(pr + 1) * q]
            rhs = jnp.concatenate(
                [jnp.where(first_head, xp, 0.0).astype(bf16), jnp.where(first_head, 0.0, xp).astype(bf16)], axis=0)
            lhs = []
            for k in range(2):
                i = 2 * pr + k
                h = g * SSD_HEADS_PER_GROUP + i
                seg = acol[i] - acum_t_ref[h:h + 1, :]
                decay = jnp.exp2(jnp.where(tril, seg, -jnp.inf))
                lhs.append((cb * decay).astype(bf16))
            y_pairs.append(_dot(jnp.concatenate(lhs, axis=1), rhs))
        y = y + jnp.concatenate(y_pairs, axis=1) + xg * dskip_ref[:, g * gw:(g + 1) * gw]
        o_ref[:, g * gw:(g + 1) * gw] = y.astype(o_ref.dtype)


def _ssd_scan(xbc, dt, acum, acum_t, d_skip, batch, seq):
    m = xbc.shape[0]
    q = SSD_CHUNK
    nc = seq // q
    di = SSD_D_INNER
    bw = SSD_BC_WIDTH
    chunk = lambda b, c: (b * nc + c, 0)
    fixed = lambda b, c: (0, 0)
    d_cols = jnp.repeat(d_skip, SSD_HEAD_DIM).reshape(1, di)
    return pl.pallas_call(
        _ssd_scan_kernel,
        grid=(batch, nc),
        in_specs=[
            pl.BlockSpec((q, di), chunk),
            pl.BlockSpec((q, bw), lambda b, c: (b * nc + c, di // bw)),
            pl.BlockSpec((q, bw), lambda b, c: (b * nc + c, di // bw + 1)),
            pl.BlockSpec((q, LANES), chunk),
            pl.BlockSpec((q, LANES), chunk),
            pl.BlockSpec((SSD_N_HEADS, q), lambda b, c: (0, b * nc + c)),
            pl.BlockSpec((1, di), fixed),
        ],
        out_specs=pl.BlockSpec((q, di), chunk),
        out_shape=jax.ShapeDtypeStruct((m, di), bf16),
        scratch_shapes=[pltpu.VMEM((SSD_N_GROUPS, SSD_D_STATE, SSD_GROUP_WIDTH), f32),
                        pltpu.VMEM((4 * SSD_N_HEADS, di), bf16)],
        compiler_params=_params("parallel", "arbitrary"),
        name="ssd_scan",
    )(xbc, xbc, xbc, dt, acum, acum_t, d_cols)


def _fox_gate_kernel(hn_ref, w_ref, b_row_ref, csum_ref):
    hn = hn_ref[...]
    seq = hn.shape[0]
    q = LANES
    logf = -_softplus(-(_dot_nt(hn, _narrow_weight(w_ref, FOX_N_HEADS)) + b_row_ref[...]))
    tril = _tri(q, lower=True)
    carry = jnp.zeros((1, LANES), f32)
    for c in range(seq // q):
        hi, mid, lo = _split3(logf[c * q:(c + 1) * q, :])
        cs = _dot(tril, hi) + _dot(tril, mid) + _dot(tril, lo) + carry
        csum_ref[c * q:(c + 1) * q, :] = cs * LOG2E
        carry = cs[q - 1:q, :]


def _fox_gate(hn, w_stack, layer, b_f, batch, seq):
    m, d = hn.shape
    nh = FOX_N_HEADS
    return pl.pallas_call(
        _fox_gate_kernel,
        grid=(batch,),
        in_specs=[
            pl.BlockSpec((seq, d), lambda b: (b, 0)),
            pl.BlockSpec((None, LANES, d), lambda b: (layer, 3 * FOX_WIDTH // LANES, 0)),
            pl.BlockSpec((1, LANES), lambda b: (0, 0)),
        ],
        out_specs=pl.BlockSpec((seq, LANES), lambda b: (b, 0)),
        out_shape=jax.ShapeDtypeStruct((m, LANES), f32),
        compiler_params=_params("parallel"),
        name="fox_gate",
    )(hn, w_stack, jnp.pad(b_f, (0, LANES - nh)).reshape(1, LANES))


ATTN_BLOCK = 512


def _fox_attn_kernel(q_ref, k_ref, v_ref, csum_ref, o_ref, qa_ref, ka_ref, va_ref):
    seq = q_ref.shape[0]
    t = ATTN_BLOCK
    hd = FOX_HEAD_DIM
    head = pl.program_id(1)
    lane = lax.broadcasted_iota(jnp.int32, (seq, LANES), 1)
    c = jnp.sum(jnp.where(lane == head, csum_ref[...], 0.0), axis=1, keepdims=True)
    hi = c.astype(bf16).astype(f32)
    r1 = c - hi
    mid = r1.astype(bf16).astype(f32)
    lo = r1 - mid
    ext_q = jnp.where(lane == 0, hi, jnp.where(lane == 1, mid, jnp.where(lane == 2, lo,
                                                                         jnp.where(lane < 6, 1.0, 0.0))))
    ext_k = jnp.where(lane < 3, 1.0, jnp.where(lane == 3, -hi, jnp.where(lane == 4, -mid,
                                                                         jnp.where(lane == 5, -lo, 0.0))))
    qa_ref[:, :hd] = q_ref[...]
    qa_ref[:, hd:] = ext_q.astype(bf16)
    ka_ref[:, :hd] = k_ref[...]
    ka_ref[:, hd:] = ext_k.astype(bf16)
    va_ref[:, :hd] = v_ref[...]
    va_ref[:, hd:] = jnp.ones((seq, hd), bf16)

    causal = lax.broadcasted_iota(jnp.int32, (t, t), 0) >= lax.broadcasted_iota(jnp.int32, (t, t), 1)
    for qi in range(seq // t):
        q = qa_ref[qi * t:(qi + 1) * t, :]
        for j in range(qi + 1):
            s = _dot_nt(q, ka_ref[j * t:(j + 1) * t, :])
            if j == qi:
                s = jnp.where(causal, s, -jnp.inf)
            v = va_ref[j * t:(j + 1) * t, :]
            m_blk = jnp.max(s, axis=1, keepdims=True)
            if j == 0:
                m_run = m_blk
                acc = _dot(jnp.exp2(s - m_run).astype(bf16), v)
            else:
                m_new = jnp.maximum(m_run, m_blk)
                acc = jnp.exp2(m_run - m_new) * acc + _dot(jnp.exp2(s - m_new).astype(bf16), v)
                m_run = m_new
        o_ref[qi * t:(qi + 1) * t, :] = (acc[:, :hd] * (1.0 / acc[:, hd:hd + 1])).astype(o_ref.dtype)


def _fox_attention(qkv, csum, batch, seq):
    m = qkv.shape[0]
    nh = FOX_N_HEADS
    hd = FOX_HEAD_DIM
    return pl.pallas_call(
        _fox_attn_kernel,
        grid=(batch, nh),
        in_specs=[
            pl.BlockSpec((seq, hd), lambda b, h: (b, h)),
            pl.BlockSpec((seq, hd), lambda b, h: (b, nh + h)),
            pl.BlockSpec((seq, hd), lambda b, h: (b, 2 * nh + h)),
            pl.BlockSpec((seq, LANES), lambda b, h: (b, 0)),
        ],
        out_specs=pl.BlockSpec((seq, hd), lambda b, h: (b, h)),
        out_shape=jax.ShapeDtypeStruct((m, FOX_WIDTH), bf16),
        scratch_shapes=[pltpu.VMEM((seq, 2 * hd), bf16)] * 3,
        compiler_params=_params("parallel", "parallel"),
        name="fox_attn",
    )(qkv, qkv, qkv, csum)


def kernel(x, p, norm_mix_pre, norm_mix_post, norm_ffn_pre, norm_ffn_post, ssd_w_in, ssd_conv_w, ssd_conv_b, ssd_dt_bias, ssd_a_log, ssd_d, ssd_norm_w, ssd_w_out, fox_w_in, fox_b_f, fox_w_out, ffn_w_gate, ffn_w_up, ffn_w_down, ple_w_proj, ple_norm, ple_w_gate):
    batch, seq, d = x.shape
    depth = p.shape[0]
    m = batch * seq
    h = x.reshape(m, d)
    p_stack = p.reshape(depth, m, PLE_DIM)
    ssd_scale = jnp.ones((1, SSD_D_INNER), f32)
    fox_scale = jnp.concatenate(
        [jnp.full((1, FOX_WIDTH), FOX_HEAD_DIM ** -0.5 * LOG2E, f32), jnp.ones((1, 2 * FOX_WIDTH), f32)], axis=1)

    ssd_w_in = jnp.swapaxes(ssd_w_in, 1, 2)
    fox_w_in = jnp.swapaxes(fox_w_in, 1, 2)

    hn = None
    for i in range(depth):
        j = i // 2
        if i % 2 == 0:
            if i == 0:
                hn, dt, acum, acum_t = _ssd_dt(h, ssd_w_in, j, ssd_dt_bias[j], ssd_a_log[j],
                                               pre_norm_gain=norm_mix_pre[0:1])
            else:
                dt, acum, acum_t = _ssd_dt(hn, ssd_w_in, j, ssd_dt_bias[j], ssd_a_log[j])
            z = _in_proj(hn, ssd_w_in, j, SSD_D_INNER, ssd_scale, "ssd_in_proj_z")
            xbc = _in_proj_conv(hn, ssd_w_in, ssd_conv_w, ssd_conv_b, j, SSD_D_INNER, SSD_CONV_DIM, seq)
            y = _ssd_scan(xbc, dt, acum, acum_t, ssd_d[j], batch, seq)
            h, hn = _ssd_out(y, z, ssd_norm_w, ssd_w_out, j, h, norm_mix_post, norm_ffn_pre, i)
        else:
            qkv = _in_proj(hn, fox_w_in, j, 3 * FOX_WIDTH, fox_scale, "fox_in_proj")
            csum = _fox_gate(hn, fox_w_in, j, fox_b_f[j], batch, seq)
            mixed = _fox_attention(qkv, csum, batch, seq)
            h, hn = _matmul_residual(mixed, fox_w_out, j, h, norm_mix_post, i, "fox_out_proj", gnext=norm_ffn_pre,
                                     tm=512)
        act = _ffn_up(hn, ffn_w_gate, ffn_w_up, i)
        (h,) = _matmul_residual(act, ffn_w_down, i, h, norm_ffn_post, i, "ffn_down")
        if i == depth - 1:
            (h,) = _ple(h, p_stack, ple_w_gate, ple_w_proj, ple_norm, i)
        else:
            h, hn = _ple(h, p_stack, ple_w_gate, ple_w_proj, ple_norm, i, gnext=norm_mix_pre, next_layer=i + 1)
    return h.reshape(batch, seq, d)
```
